```python
import jax, jax.numpy as jnp
from jax import lax
import numpy as np

D_MODEL = 1024
BATCH = 16
SEQ = 2048
DEPTH = 2

MEM_LEN = 256
RET_HEADS = 4
RET_DK = 64
RET_DV = 128
RET_CHUNK = 128
GLA_HEADS = 4
GLA_DK = 64
GLA_DV = 128
GLA_CHUNK = 64
GLA_GATE_RANK = 16
GLA_GATE_TAU = 16.0
XA_HEADS = 4
XA_DH = D_MODEL // XA_HEADS
D_FF = 2816
ROPE_BASE = 10000.0
EPS = 1e-6

RET_W = RET_HEADS * RET_DV
GLA_W = GLA_HEADS * GLA_DV
MIX_W = RET_W + GLA_W
IN_SIZES = (
    RET_HEADS * RET_DK,
    RET_HEADS * RET_DK,
    RET_W,
    RET_W,
    GLA_HEADS * GLA_DK,
    GLA_HEADS * GLA_DK,
    GLA_W,
    GLA_GATE_RANK,
    GLA_W,
)
IN_DIM = sum(IN_SIZES)

kernel_name = "hybrid_retention_gla_macaron_block"


def _split_points():
    pts, acc = [], 0
    for s in IN_SIZES[:-1]:
        acc += s
        pts.append(acc)
    return pts


def rmsnorm(x, g):
    xf = x.astype(jnp.float32)
    y = xf * lax.rsqrt(jnp.mean(xf * xf, axis=-1, keepdims=True) + EPS)
    return (y * g.astype(jnp.float32)).astype(x.dtype)


def swiglu(h, w_gate, w_up, w_down):
    return (jax.nn.silu(h @ w_gate) * (h @ w_up)) @ w_down


def rope(x, cos, sin):
    half = x.shape[-1] // 2
    x1, x2 = x[..., :half], x[..., half:]
    c, s = cos.astype(x.dtype), sin.astype(x.dtype)
    return jnp.concatenate([x1 * c - x2 * s, x1 * s + x2 * c], axis=-1)


def to_chunks(x, c):
    b, t, h, d = x.shape
    return x.reshape(b, t // c, c, h, d).transpose(1, 0, 3, 2, 4)


def from_chunks(o):
    n, b, h, c, d = o.shape
    return o.transpose(1, 0, 3, 2, 4).reshape(b, n * c, h, d)


def retention(q, k, v):
    b, t, h, dk = q.shape
    dv = v.shape[-1]
    c = RET_CHUNK
    log_gamma = jnp.log1p(-jnp.exp2(-5.0 - jnp.arange(h, dtype=jnp.float32)))
    idx = jnp.arange(c, dtype=jnp.float32)
    rel = idx[:, None] - idx[None, :]
    causal = rel >= 0
    decay_in = jnp.where(causal[None], jnp.exp(jnp.where(causal, rel, 0.0)[None] * log_gamma[:, None, None]), 0.0)
    q_dec = jnp.exp((idx + 1.0)[None, :] * log_gamma[:, None])
    k_dec = jnp.exp((c - 1.0 - idx)[None, :] * log_gamma[:, None])
    chunk_dec = jnp.exp(c * log_gamma)

    def step(state, inp):
        qc, kc, vc = inp
        scores = jnp.einsum('bhid,bhjd->bhij', qc, kc) * decay_in[None]
        o = (jnp.einsum('bhij,bhjv->bhiv', scores, vc)
             + jnp.einsum('bhid,bhdv->bhiv', qc * q_dec[None, :, :, None], state))
        state = (chunk_dec[None, :, None, None] * state
                 + jnp.einsum('bhjd,bhjv->bhdv', kc * k_dec[None, :, :, None], vc))
        return state, o

    s0 = jnp.zeros((b, h, dk, dv), jnp.float32)
    _, o = lax.scan(step, s0, (to_chunks(q, c), to_chunks(k, c), to_chunks(v, c)))
    return from_chunks(o)


def gla(q, k, v, log_a):
    b, t, h, dk = q.shape
    dv = v.shape[-1]
    c = GLA_CHUNK
    mask = jnp.tril(jnp.ones((c, c), dtype=bool))[:, :, None]

    def step(state, inp):
        qc, kc, vc, gc = inp
        cum = jnp.cumsum(gc, axis=-2)
        diff = cum[:, :, :, None, :] - cum[:, :, None, :, :]
        decay = jnp.exp(jnp.where(mask, diff, -jnp.inf))
        scores = jnp.einsum('bhid,bhjd,bhijd->bhij', qc, kc, decay)
        o = (jnp.einsum('bhij,bhjv->bhiv', scores, vc)
             + jnp.einsum('bhid,bhdv->bhiv', qc * jnp.exp(cum), state))
        last = cum[:, :, -1:, :]
        state = (jnp.exp(last[:, :, 0, :])[..., None] * state
                 + jnp.einsum('bhjd,bhjv->bhdv', kc * jnp.exp(last - cum), vc))
        return state, o

    s0 = jnp.zeros((b, h, dk, dv), jnp.float32)
    _, o = lax.scan(step, s0, (to_chunks(q, c), to_chunks(k, c), to_chunks(v, c), to_chunks(log_a, c)))
    return from_chunks(o)


def hybrid_mixer(h, cos, sin, w_in, w_a2, b_a2, ret_norm_g, gla_norm_g, w_out):
    bsz, t, _ = h.shape
    z = h @ w_in
    rq, rk, rv, rg, gq, gk, gv, ga, gg = jnp.split(z, _split_points(), axis=-1)
    rq = rope(rq.reshape(bsz, t, RET_HEADS, RET_DK), cos, sin)
    rk = rope(rk.reshape(bsz, t, RET_HEADS, RET_DK), cos, sin) * (RET_DK ** -0.5)
    ro = retention(rq, rk, rv.reshape(bsz, t, RET_HEADS, RET_DV))
    ro = rmsnorm(ro, ret_norm_g).astype(h.dtype).reshape(bsz, t, RET_W) * jax.nn.silu(rg)
    log_a = jax.nn.log_sigmoid((ga @ w_a2 + b_a2).astype(jnp.float32)) / GLA_GATE_TAU
    log_a = log_a.reshape(bsz, t, GLA_HEADS, GLA_DK)
    go = gla(gq.reshape(bsz, t, GLA_HEADS, GLA_DK) * (GLA_DK ** -0.5),
             gk.reshape(bsz, t, GLA_HEADS, GLA_DK),
             gv.reshape(bsz, t, GLA_HEADS, GLA_DV), log_a)
    go = rmsnorm(go, gla_norm_g).astype(h.dtype).reshape(bsz, t, GLA_W) * jax.nn.silu(gg)
    return jnp.concatenate([ro, go], axis=-1) @ w_out


def memory_cross_attn(h, mem_n, w_q, w_kv, w_o):
    bsz, t, _ = h.shape
    m = mem_n.shape[1]
    q = (h @ w_q).reshape(bsz, t, XA_HEADS, XA_DH)
    k, v = jnp.split(mem_n @ w_kv, 2, axis=-1)
    k = k.reshape(bsz, m, XA_HEADS, XA_DH)
    v = v.reshape(bsz, m, XA_HEADS, XA_DH)
    s = jnp.einsum('bthd,bmhd->bhtm', q, k).astype(jnp.float32) * (XA_DH ** -0.5)
    p = jax.nn.softmax(s, axis=-1).astype(v.dtype)
    o = jnp.einsum('bhtm,bmhd->bthd', p, v).reshape(bsz, t, D_MODEL)
    return o @ w_o


def setup_inputs(seed: int = 0) -> dict:
    key = jax.random.key(seed)
    ks = iter(jax.random.split(key, 40))
    f32 = jnp.float32

    def w(shape, fan_in):
        return jax.random.normal(next(ks), (DEPTH,) + shape, f32) * (fan_in ** -0.5)

    def gain(shape):
        return 1.0 + 0.02 * jax.random.normal(next(ks), (DEPTH,) + shape, f32)

    x = jax.random.normal(next(ks), (BATCH, SEQ, D_MODEL), f32)
    mem = jax.random.normal(next(ks), (BATCH, MEM_LEN, D_MODEL), f32)
    offset = jax.random.randint(next(ks), (BATCH, 1), 0, 4096, dtype=jnp.int32)
    positions = offset + jnp.arange(SEQ, dtype=jnp.int32)[None, :]
    return {
        "x": x, "mem": mem, "positions": positions,
        "ffn1_pre_g": gain((D_MODEL,)),
        "ffn1_w_gate": w((D_MODEL, D_FF), D_MODEL),
        "ffn1_w_up": w((D_MODEL, D_FF), D_MODEL),
        "ffn1_w_down": w((D_FF, D_MODEL), D_FF),
        "ffn1_post_g": gain((D_MODEL,)),
        "mix_pre_g": gain((D_MODEL,)),
        "w_in": w((D_MODEL, IN_DIM), D_MODEL),
        "w_a2": w((GLA_GATE_RANK, GLA_HEADS * GLA_DK), GLA_GATE_RANK),
        "b_a2": 0.01 * jax.random.normal(next(ks), (DEPTH, GLA_HEADS * GLA_DK), f32),
        "ret_norm_g": gain((RET_HEADS, RET_DV)),
        "gla_norm_g": gain((GLA_HEADS, GLA_DV)),
        "w_out": w((MIX_W, D_MODEL), MIX_W),
        "mix_post_g": gain((D_MODEL,)),
        "xa_pre_g": gain((D_MODEL,)),
        "xa_mem_g": gain((D_MODEL,)),
        "xa_w_q": w((D_MODEL, D_MODEL), D_MODEL),
        "xa_w_kv": w((D_MODEL, 2 * D_MODEL), D_MODEL),
        "xa_w_o": w((D_MODEL, D_MODEL), D_MODEL),
        "xa_post_g": gain((D_MODEL,)),
        "ffn2_pre_g": gain((D_MODEL,)),
        "ffn2_w_gate": w((D_MODEL, D_FF), D_MODEL),
        "ffn2_w_up": w((D_MODEL, D_FF), D_MODEL),
        "ffn2_w_down": w((D_FF, D_MODEL), D_FF),
        "ffn2_post_g": gain((D_MODEL,)),
    }


def reference(x, mem, positions,
              ffn1_pre_g, ffn1_w_gate, ffn1_w_up, ffn1_w_down, ffn1_post_g,
              mix_pre_g, w_in, w_a2, b_a2, ret_norm_g, gla_norm_g, w_out, mix_post_g,
              xa_pre_g, xa_mem_g, xa_w_q, xa_w_kv, xa_w_o, xa_post_g,
              ffn2_pre_g, ffn2_w_gate, ffn2_w_up, ffn2_w_down, ffn2_post_g):
    inv_freq = ROPE_BASE ** (-jnp.arange(0, RET_DK, 2, dtype=jnp.float32) / RET_DK)
    ang = positions.astype(jnp.float32)[..., None] * inv_freq
    cos = jnp.cos(ang)[:, :, None, :]
    sin = jnp.sin(ang)[:, :, None, :]

    for l in range(DEPTH):
        x = x + 0.5 * rmsnorm(swiglu(rmsnorm(x, ffn1_pre_g[l]), ffn1_w_gate[l], ffn1_w_up[l], ffn1_w_down[l]), ffn1_post_g[l])
        x = x + rmsnorm(hybrid_mixer(rmsnorm(x, mix_pre_g[l]), cos, sin, w_in[l], w_a2[l], b_a2[l],
                                     ret_norm_g[l], gla_norm_g[l], w_out[l]), mix_post_g[l])
        x = x + rmsnorm(memory_cross_attn(rmsnorm(x, xa_pre_g[l]), rmsnorm(mem, xa_mem_g[l]),
                                          xa_w_q[l], xa_w_kv[l], xa_w_o[l]), xa_post_g[l])
        x = x + 0.5 * rmsnorm(swiglu(rmsnorm(x, ffn2_pre_g[l]), ffn2_w_gate[l], ffn2_w_up[l], ffn2_w_down[l]), ffn2_post_g[l])
    return x
```

```python
import functools

import numpy as np
import jax
import jax.numpy as jnp
from jax import lax
from jax.experimental import pallas as pl
from jax.experimental.pallas import tpu as pltpu

D_MODEL = 1024
D_FF = 2816
EPS = 1e-6
ROPE_BASE = 10000.0

HEADS = 4
DK = 64
DV = 128
QK_W = HEADS * DK
V_W = HEADS * DV
RET_CHUNK = 128
GLA_CHUNK = 64
GLA_SUB = 16
N_SUB = GLA_CHUNK // GLA_SUB
GATE_RANK = 16
GATE_RANK_PAD = 128
GATE_TAU = 16.0
QK_SCALE = DK ** -0.5

XA_HEADS = 4
XA_DH = D_MODEL // XA_HEADS
MEM_LEN = 256

OFF_RQ = 0
OFF_RK = OFF_RQ + QK_W
OFF_RV = OFF_RK + QK_W
OFF_RG = OFF_RV + V_W
OFF_GQ = OFF_RG + V_W
OFF_GK = OFF_GQ + QK_W
OFF_GV = OFF_GK + QK_W
OFF_GG = OFF_GV + V_W
OFF_GA = OFF_GG + V_W
IN_W = OFF_GA + GATE_RANK_PAD

FFN_TILE = 512
FFN_CHUNK = 256
MIX_TILE = 512
XA_TILE = 512
ROPE_TILE = 1024
VMEM_LIMIT = 56 * 1024 * 1024

F32 = jnp.float32
BF16 = jnp.bfloat16


def _rms(x, g):
    return x * lax.rsqrt(jnp.mean(x * x, axis=-1, keepdims=True) + EPS) * g


def _silu(x):
    return x * (1.0 / (1.0 + jnp.exp(-x)))


def _dot(a, b):
    return jnp.dot(a, b, preferred_element_type=F32)


def _dot_nt(a, b):
    return lax.dot_general(a, b, (((1,), (1,)), ((), ())), preferred_element_type=F32)


def _dot_tn(a, b):
    return lax.dot_general(a, b, (((0,), (0,)), ((), ())), preferred_element_type=F32)


def _ffn_kernel(x_ref, pre_g_ref, wg_ref, wu_ref, wd_ref, post_g_ref, o_ref):
    x = x_ref[...]
    h = _rms(x, pre_g_ref[...]).astype(BF16)
    acc = jnp.zeros(x.shape, F32)
    for c in range(D_FF // FFN_CHUNK):
        lo = c * FFN_CHUNK
        a = _dot(h, wg_ref[:, lo:lo + FFN_CHUNK])
        b = _dot(h, wu_ref[:, lo:lo + FFN_CHUNK])
        u = (_silu(a) * b).astype(BF16)
        acc = acc + _dot(u, wd_ref[lo:lo + FFN_CHUNK, :])
    o_ref[...] = x + 0.5 * _rms(acc, post_g_ref[...])


def _ffn(x2d, pre_g, wg, wu, wd, post_g):
    n = x2d.shape[0]
    const = lambda i: (0, 0)
    return pl.pallas_call(
        _ffn_kernel,
        grid=(n // FFN_TILE,),
        in_specs=[
            pl.BlockSpec((FFN_TILE, D_MODEL), lambda i: (i, 0)),
            pl.BlockSpec((1, D_MODEL), const),
            pl.BlockSpec((D_MODEL, D_FF), const),
            pl.BlockSpec((D_MODEL, D_FF), const),
            pl.BlockSpec((D_FF, D_MODEL), const),
            pl.BlockSpec((1, D_MODEL), const),
        ],
        out_specs=pl.BlockSpec((FFN_TILE, D_MODEL), lambda i: (i, 0)),
        out_shape=jax.ShapeDtypeStruct(x2d.shape, F32),
        compiler_params=pltpu.CompilerParams(
            dimension_semantics=("arbitrary",), vmem_limit_bytes=VMEM_LIMIT),
        name="ffn",
    )(x2d, pre_g, wg, wu, wd, post_g)


def _rope_kernel(pos_ref, freq_ref, cos_ref, sin_ref):
    ang = pos_ref[...] * freq_ref[...]
    cos_ref[...] = jnp.cos(ang)
    sin_ref[...] = jnp.sin(ang)


def _rope_tables(positions):
    b, t = positions.shape
    pos = positions.astype(F32).reshape(b * t, 1)
    inv_freq = ROPE_BASE ** (-jnp.arange(0, DK, 2, dtype=F32) / DK)
    freq = jnp.tile(inv_freq, HEADS).reshape(1, HEADS * DK // 2)
    n = b * t
    out = jax.ShapeDtypeStruct((n, HEADS * DK // 2), F32)
    return pl.pallas_call(
        _rope_kernel,
        grid=(n // ROPE_TILE,),
        in_specs=[
            pl.BlockSpec((ROPE_TILE, 1), lambda i: (i, 0)),
            pl.BlockSpec((1, HEADS * DK // 2), lambda i: (0, 0)),
        ],
        out_specs=[pl.BlockSpec((ROPE_TILE, HEADS * DK // 2), lambda i: (i, 0))] * 2,
        out_shape=[out, out],
        compiler_params=pltpu.CompilerParams(dimension_semantics=("arbitrary",)),
        name="rope_tables",
    )(pos, freq)


def _mixer_kernel(x_ref, cos_ref, sin_ref, pre_g_ref, w_in_ref, w_a2_ref, b_a2_ref,
                  ret_g_ref, gla_g_ref, w_out_ref, post_g_ref,
                  dmat_ref, qdec_ref, kdec_ref, cdec_ref,
                  o_ref,
                  z_ref, la_ref, y_ref, rstate_ref, gstate_ref):
    tile = x_ref.shape[0]

    @pl.when(pl.program_id(1) == 0)
    def _():
        rstate_ref[...] = jnp.zeros(rstate_ref.shape, F32)
        gstate_ref[...] = jnp.zeros(gstate_ref.shape, F32)

    x = x_ref[...]
    h = _rms(x, pre_g_ref[...]).astype(BF16)
    z_ref[...] = _dot(h, w_in_ref[...])

    ga = z_ref[:, OFF_GA:OFF_GA + GATE_RANK_PAD].astype(BF16)
    pre = _dot(ga, w_a2_ref[...]) + b_a2_ref[...]
    la_ref[...] = (jnp.minimum(pre, 0.0) - jnp.log1p(jnp.exp(-jnp.abs(pre)))) * (1.0 / GATE_TAU)

    lane_qk = lax.broadcasted_iota(jnp.int32, (1, QK_W), 1)
    row_blk = lax.broadcasted_iota(jnp.int32, (V_W, QK_W), 0) // DV

    ret_head_of_lane = (lane_qk % (QK_W // 2)) // (DK // 2)
    ret_masks = [ret_head_of_lane == hd for hd in range(HEADS)]
    ret_bd = (lax.broadcasted_iota(jnp.int32, (V_W, QK_W), 1) % (QK_W // 2)) // (DK // 2) == row_blk
    half = QK_W // 2

    def ret_chunk(c, carry):
        r0 = pl.multiple_of(c * RET_CHUNK, RET_CHUNK)
        rows = pl.ds(r0, RET_CHUNK)
        cosv = cos_ref[rows, :]
        sinv = sin_ref[rows, :]

        def rope(off):
            x1 = z_ref[rows, off:off + half]
            x2 = z_ref[rows, off + half:off + QK_W]
            return jnp.concatenate([x1 * cosv - x2 * sinv, x1 * sinv + x2 * cosv], axis=-1)

        q = rope(OFF_RQ)
        k = rope(OFF_RK) * QK_SCALE
        v = z_ref[rows, OFF_RV:OFF_RV + V_W].astype(BF16)
        qstack = jnp.concatenate([jnp.where(m, q, 0.0) for m in ret_masks], axis=0).astype(BF16)
        s_all = _dot_nt(qstack, k.astype(BF16)) * dmat_ref[...]
        st = rstate_ref[...]
        o_inter = _dot_nt((q * qdec_ref[...]).astype(BF16), st.astype(BF16))
        kd = (k * kdec_ref[...]).astype(BF16)
        upd = _dot_tn(v, kd)
        rstate_ref[...] = st * cdec_ref[...] + jnp.where(ret_bd, upd, 0.0)
        for hd in range(HEADS):
            s_h = s_all[hd * RET_CHUNK:(hd + 1) * RET_CHUNK].astype(BF16)
            o = _dot(s_h, v[:, hd * DV:(hd + 1) * DV]) + o_inter[:, hd * DV:(hd + 1) * DV]
            o = _rms(o, ret_g_ref[:, hd * DV:(hd + 1) * DV])
            gate = _silu(z_ref[rows, OFF_RG + hd * DV:OFF_RG + (hd + 1) * DV])
            y_ref[rows, hd * DV:(hd + 1) * DV] = (o * gate).astype(BF16)
        return carry

    lax.fori_loop(0, tile // RET_CHUNK, ret_chunk, 0)

    lane_cat = lax.broadcasted_iota(jnp.int32, (1, N_SUB * QK_W), 1)
    gla_masks4 = [(lane_cat % QK_W) // DK == hd for hd in range(HEADS)]
    gla_bd = lax.broadcasted_iota(jnp.int32, (V_W, QK_W), 1) // DK == row_blk
    ri = lax.broadcasted_iota(jnp.int32, (GLA_CHUNK, GLA_CHUNK), 0)
    ci = lax.broadcasted_iota(jnp.int32, (GLA_CHUNK, GLA_CHUNK), 1)
    tri_bf = jnp.where(ri >= ci, 1.0, 0.0).astype(BF16)
    ri4 = lax.broadcasted_iota(jnp.int32, (HEADS * GLA_CHUNK, GLA_CHUNK), 0) % GLA_CHUNK
    ci4 = lax.broadcasted_iota(jnp.int32, (HEADS * GLA_CHUNK, GLA_CHUNK), 1)
    causal4 = ri4 >= ci4
    sub_of_row = lax.broadcasted_iota(jnp.int32, (GLA_CHUNK, 1), 0) // GLA_SUB

    def gla_chunk(c, carry):
        r0 = pl.multiple_of(c * GLA_CHUNK, GLA_CHUNK)
        rows = pl.ds(r0, GLA_CHUNK)
        la = la_ref[rows, :]
        a1 = la.astype(BF16)
        r1 = la - a1.astype(F32)
        a2 = r1.astype(BF16)
        a3 = (r1 - a2.astype(F32)).astype(BF16)
        cum = _dot(tri_bf, a1) + _dot(tri_bf, a2) + _dot(tri_bf, a3)
        gq = z_ref[rows, OFF_GQ:OFF_GQ + QK_W] * QK_SCALE
        gk = z_ref[rows, OFF_GK:OFF_GK + QK_W]
        v = z_ref[rows, OFF_GV:OFF_GV + V_W].astype(BF16)

        ends = [cum[(s + 1) * GLA_SUB - 1:(s + 1) * GLA_SUB, :] for s in range(N_SUB)]
        last = ends[-1]
        end_of_row = jnp.concatenate(
            [jnp.broadcast_to(e, (GLA_SUB, QK_W)) for e in ends], axis=0)
        kt = gk * jnp.exp(end_of_row - cum)
        kb = jnp.concatenate([jnp.where(sub_of_row == s, kt, 0.0) for s in range(N_SUB)],
                             axis=-1).astype(BF16)
        qcat = jnp.concatenate(
            [gq * jnp.exp(jnp.where(sub_of_row >= s, cum - ends[s], 0.0)) for s in range(N_SUB)],
            axis=-1)
        qstack = jnp.concatenate([jnp.where(m, qcat, 0.0) for m in gla_masks4], axis=0).astype(BF16)
        s_all = jnp.where(causal4, _dot_nt(qstack, kb), 0.0)

        st = gstate_ref[...]
        o_inter = _dot_nt((gq * jnp.exp(cum)).astype(BF16), st.astype(BF16))
        kd = (gk * jnp.exp(last - cum)).astype(BF16)
        upd = _dot_tn(v, kd)
        gstate_ref[...] = st * jnp.exp(last) + jnp.where(gla_bd, upd, 0.0)
        for hd in range(HEADS):
            s_h = s_all[hd * GLA_CHUNK:(hd + 1) * GLA_CHUNK].astype(BF16)
            o = _dot(s_h, v[:, hd * DV:(hd + 1) * DV]) + o_inter[:, hd * DV:(hd + 1) * DV]
            o = _rms(o, gla_g_ref[:, hd * DV:(hd + 1) * DV])
            gate = _silu(z_ref[rows, OFF_GG + hd * DV:OFF_GG + (hd + 1) * DV])
            y_ref[rows, V_W + hd * DV:V_W + (hd + 1) * DV] = (o * gate).astype(BF16)
        return carry

    lax.fori_loop(0, tile // GLA_CHUNK, gla_chunk, 0)

    out = _dot(y_ref[...], w_out_ref[...])
    o_ref[...] = x + _rms(out, post_g_ref[...])


def _mixer(x, cos_t, sin_t, pre_g, w_in, w_a2, b_a2, ret_g, gla_g, w_out, post_g, tables):
    b, t, d = x.shape
    dmat, qdec, kdec, cdec = tables
    const2 = lambda bi, ti: (0, 0)
    tok = lambda bi, ti: (bi, ti, 0)
    return pl.pallas_call(
        _mixer_kernel,
        grid=(b, t // MIX_TILE),
        in_specs=[
            pl.BlockSpec((None, MIX_TILE, d), tok),
            pl.BlockSpec((None, MIX_TILE, QK_W // 2), tok),
            pl.BlockSpec((None, MIX_TILE, QK_W // 2), tok),
            pl.BlockSpec((1, d), const2),
            pl.BlockSpec((d, IN_W), const2),
            pl.BlockSpec((GATE_RANK_PAD, QK_W), const2),
            pl.BlockSpec((1, QK_W), const2),
            pl.BlockSpec((1, V_W), const2),
            pl.BlockSpec((1, V_W), const2),
            pl.BlockSpec((2 * V_W, d), const2),
            pl.BlockSpec((1, d), const2),
            pl.BlockSpec((HEADS * RET_CHUNK, RET_CHUNK), const2),
            pl.BlockSpec((RET_CHUNK, QK_W), const2),
            pl.BlockSpec((RET_CHUNK, QK_W), const2),
            pl.BlockSpec((1, QK_W), const2),
        ],
        out_specs=pl.BlockSpec((None, MIX_TILE, d), tok),
        out_shape=jax.ShapeDtypeStruct(x.shape, F32),
        scratch_shapes=[
            pltpu.VMEM((MIX_TILE, IN_W), F32),
            pltpu.VMEM((MIX_TILE, QK_W), F32),
            pltpu.VMEM((MIX_TILE, 2 * V_W), BF16),
            pltpu.VMEM((V_W, QK_W), F32),
            pltpu.VMEM((V_W, QK_W), F32),
        ],
        compiler_params=pltpu.CompilerParams(
            dimension_semantics=("arbitrary", "arbitrary"), vmem_limit_bytes=VMEM_LIMIT),
        name="mixer",
    )(x, cos_t, sin_t, pre_g, w_in, w_a2, b_a2, ret_g, gla_g, w_out, post_g,
      dmat, qdec, kdec, cdec)


def _retention_tables():
    c = RET_CHUNK
    log_gamma = jnp.log1p(-jnp.exp2(-5.0 - jnp.arange(HEADS, dtype=F32)))
    idx = jnp.arange(c, dtype=F32)
    rel = idx[:, None] - idx[None, :]
    causal = rel >= 0
    decay_in = jnp.where(causal[None],
                         jnp.exp(jnp.where(causal, rel, 0.0)[None] * log_gamma[:, None, None]), 0.0)
    q_dec = jnp.exp((idx + 1.0)[None, :] * log_gamma[:, None])
    k_dec = jnp.exp((c - 1.0 - idx)[None, :] * log_gamma[:, None])
    chunk_dec = jnp.exp(c * log_gamma)
    head_of_lane = (np.arange(QK_W) % (QK_W // 2)) // (DK // 2)
    dmat = decay_in.reshape(HEADS * c, c)
    qdec = q_dec.T[:, head_of_lane]
    kdec = k_dec.T[:, head_of_lane]
    cdec = chunk_dec[head_of_lane].reshape(1, QK_W)
    return dmat, qdec, kdec, cdec


def _rotary_cols(w):
    d = w.shape[0]
    return w.reshape(d, HEADS, 2, DK // 2).transpose(0, 2, 1, 3).reshape(d, QK_W)


def _relayout_w_in(w_in):
    sizes = (QK_W, QK_W, V_W, V_W, QK_W, QK_W, V_W, GATE_RANK, V_W)
    pts = np.cumsum(sizes)[:-1]
    rq, rk, rv, rg, gq, gk, gv, ga, gg = jnp.split(w_in, pts, axis=-1)
    ga = jnp.pad(ga, ((0, 0), (0, GATE_RANK_PAD - GATE_RANK)))
    return jnp.concatenate([_rotary_cols(rq), _rotary_cols(rk), rv, rg, gq, gk, gv, gg, ga], axis=-1)


def _memkv_kernel(mem_ref, g_ref, w_ref, k_ref, v_ref):
    m = _rms(mem_ref[...], g_ref[...]).astype(BF16)
    kv = _dot(m, w_ref[...])
    k_ref[...] = kv[:, :D_MODEL].astype(BF16)
    v_ref[...] = kv[:, D_MODEL:].astype(BF16)


def _memkv(mem, g, w_kv):
    b, m, d = mem.shape
    out = jax.ShapeDtypeStruct((b, m, d), BF16)
    return pl.pallas_call(
        _memkv_kernel,
        grid=(b,),
        in_specs=[
            pl.BlockSpec((None, m, d), lambda i: (i, 0, 0)),
            pl.BlockSpec((1, d), lambda i: (0, 0)),
            pl.BlockSpec((d, 2 * d), lambda i: (0, 0)),
        ],
        out_specs=[pl.BlockSpec((None, m, d), lambda i: (i, 0, 0))] * 2,
        out_shape=[out, out],
        compiler_params=pltpu.CompilerParams(
            dimension_semantics=("arbitrary",), vmem_limit_bytes=VMEM_LIMIT),
        name="memkv",
    )(mem, g, w_kv)


def _xattn_kernel(x_ref, k_ref, v_ref, pre_g_ref, wq_ref, wo_ref, post_g_ref, o_ref, att_ref):
    x = x_ref[...]
    h = _rms(x, pre_g_ref[...]).astype(BF16)
    q = _dot(h, wq_ref[...])
    for hd in range(XA_HEADS):
        cols = slice(hd * XA_DH, (hd + 1) * XA_DH)
        s = _dot_nt(q[:, cols].astype(BF16), k_ref[:, cols]) * (XA_DH ** -0.5)
        p = jnp.exp(s - jnp.max(s, axis=-1, keepdims=True))
        o = _dot(p.astype(BF16), v_ref[:, cols]) / jnp.sum(p, axis=-1, keepdims=True)
        att_ref[:, cols] = o.astype(BF16)
    out = _dot(att_ref[...], wo_ref[...])
    o_ref[...] = x + _rms(out, post_g_ref[...])


def _xattn(x, k, v, pre_g, wq, wo, post_g):
    b, t, d = x.shape
    m = k.shape[1]
    const2 = lambda bi, ti: (0, 0)
    tok = lambda bi, ti: (bi, ti, 0)
    mem = lambda bi, ti: (bi, 0, 0)
    return pl.pallas_call(
        _xattn_kernel,
        grid=(b, t // XA_TILE),
        in_specs=[
            pl.BlockSpec((None, XA_TILE, d), tok),
            pl.BlockSpec((None, m, d), mem),
            pl.BlockSpec((None, m, d), mem),
            pl.BlockSpec((1, d), const2),
            pl.BlockSpec((d, d), const2),
            pl.BlockSpec((d, d), const2),
            pl.BlockSpec((1, d), const2),
        ],
        out_specs=pl.BlockSpec((None, XA_TILE, d), tok),
        out_shape=jax.ShapeDtypeStruct(x.shape, F32),
        scratch_shapes=[pltpu.VMEM((XA_TILE, d), BF16)],
        compiler_params=pltpu.CompilerParams(
            dimension_semantics=("arbitrary", "arbitrary"), vmem_limit_bytes=VMEM_LIMIT),
        name="xattn",
    )(x, k, v, pre_g, wq, wo, post_g)


def kernel(x, mem, positions, ffn1_pre_g, ffn1_w_gate, ffn1_w_up, ffn1_w_down, ffn1_post_g, mix_pre_g, w_in, w_a2, b_a2, ret_norm_g, gla_norm_g, w_out, mix_post_g, xa_pre_g, xa_mem_g, xa_w_q, xa_w_kv, xa_w_o, xa_post_g, ffn2_pre_g, ffn2_w_gate, ffn2_w_up, ffn2_w_down, ffn2_post_g):
    b, t, d = x.shape
    depth = w_in.shape[0]
    assert d == D_MODEL and t % MIX_TILE == 0 and t % XA_TILE == 0 and (b * t) % FFN_TILE == 0
    assert (b * t) % ROPE_TILE == 0

    cos_t, sin_t = _rope_tables(positions)
    cos_t = cos_t.reshape(b, t, QK_W // 2)
    sin_t = sin_t.reshape(b, t, QK_W // 2)
    tables = _retention_tables()
    row = lambda g: g.reshape(1, -1)

    for l in range(depth):
        x = _ffn(x.reshape(b * t, d), row(ffn1_pre_g[l]), ffn1_w_gate[l].astype(BF16),
                 ffn1_w_up[l].astype(BF16), ffn1_w_down[l].astype(BF16),
                 row(ffn1_post_g[l])).reshape(b, t, d)
        w_a2_p = jnp.pad(w_a2[l], ((0, GATE_RANK_PAD - GATE_RANK), (0, 0))).astype(BF16)
        x = _mixer(x, cos_t, sin_t, row(mix_pre_g[l]), _relayout_w_in(w_in[l]).astype(BF16),
                   w_a2_p, row(b_a2[l]), row(ret_norm_g[l]), row(gla_norm_g[l]),
                   w_out[l].astype(BF16), row(mix_post_g[l]), tables)
        k_mem, v_mem = _memkv(mem, row(xa_mem_g[l]), xa_w_kv[l].astype(BF16))
        x = _xattn(x, k_mem, v_mem, row(xa_pre_g[l]), xa_w_q[l].astype(BF16),
                   xa_w_o[l].astype(BF16), row(xa_post_g[l]))
        x = _ffn(x.reshape(b * t, d), row(ffn2_pre_g[l]), ffn2_w_gate[l].astype(BF16),
                 ffn2_w_up[l].astype(BF16), ffn2_w_down[l].astype(BF16),
                 row(ffn2_post_g[l])).reshape(b, t, d)
    return x
```

```python
import numpy as np
import jax
import jax.numpy as jnp
from jax import lax
from jax.experimental import pallas as pl
from jax.experimental.pallas import tpu as pltpu

D_MODEL = 1024
D_FF = 2816
EPS = 1e-6
ROPE_BASE = 10000.0

LANES = 128
HEADS = 4
DK = 64
DV = 128
QK_W = HEADS * DK
V_W = HEADS * DV
CHUNK = 128
GLA_SUB = 32
N_SUB = CHUNK // GLA_SUB
GATE_RANK = 16
GATE_RANK_PAD = 128
GATE_TAU = 16.0
QK_SCALE = DK ** -0.5

XA_HEADS = 4
XA_DH = D_MODEL // XA_HEADS

OFF_RQ = 0
OFF_RK = OFF_RQ + QK_W
OFF_RV = OFF_RK + QK_W
OFF_RG = OFF_RV + V_W
OFF_GQ = OFF_RG + V_W
OFF_GK = OFF_GQ + 2 * QK_W
OFF_GV = OFF_GK + 2 * QK_W
OFF_GG = OFF_GV + V_W
OFF_GA = OFF_GG + V_W
IN_W = OFF_GA + GATE_RANK_PAD

FFN_TILE = 512
FFN_CHUNK = 256
MIX_TILE = 512
XA_TILE = 512
ROPE_TILE = 1024
VMEM_LIMIT = 56 * 1024 * 1024

F32 = jnp.float32
BF16 = jnp.bfloat16


def _rms(x, g):
    return x * lax.rsqrt(jnp.mean(x * x, axis=-1, keepdims=True) + EPS) * g


def _silu(x):
    return x * (1.0 / (1.0 + jnp.exp(-x)))


def _dot(a, b):
    return jnp.dot(a, b, preferred_element_type=F32)


def _dot_nt(a, b):
    return lax.dot_general(a, b, (((1,), (1,)), ((), ())), preferred_element_type=F32)


def _block_diag2(a, b):
    za = jnp.zeros(a.shape, a.dtype)
    return jnp.concatenate(
        [jnp.concatenate([a, za], axis=-1), jnp.concatenate([za, b], axis=-1)], axis=0)


def _ffn_kernel(x_ref, pre_g_ref, wg_ref, wu_ref, wd_ref, post_g_ref, o_ref):
    x = x_ref[...]
    h = _rms(x, pre_g_ref[...]).astype(BF16)
    acc = jnp.zeros(x.shape, F32)
    for c in range(D_FF // FFN_CHUNK):
        lo = c * FFN_CHUNK
        a = _dot(h, wg_ref[:, lo:lo + FFN_CHUNK])
        b = _dot(h, wu_ref[:, lo:lo + FFN_CHUNK])
        u = (_silu(a) * b).astype(BF16)
        acc = acc + _dot(u, wd_ref[lo:lo + FFN_CHUNK, :])
    o_ref[...] = x + 0.5 * _rms(acc, post_g_ref[...])


def _ffn(x2d, layer, pre_g, wg, wu, wd, post_g):
    n = x2d.shape[0]
    lay2 = lambda i: (layer, 0, 0)
    return pl.pallas_call(
        _ffn_kernel,
        grid=(n // FFN_TILE,),
        in_specs=[
            pl.BlockSpec((FFN_TILE, D_MODEL), lambda i: (i, 0)),
            pl.BlockSpec((None, 1, D_MODEL), lay2),
            pl.BlockSpec((None, D_MODEL, D_FF), lay2),
            pl.BlockSpec((None, D_MODEL, D_FF), lay2),
            pl.BlockSpec((None, D_FF, D_MODEL), lay2),
            pl.BlockSpec((None, 1, D_MODEL), lay2),
        ],
        out_specs=pl.BlockSpec((FFN_TILE, D_MODEL), lambda i: (i, 0)),
        out_shape=jax.ShapeDtypeStruct(x2d.shape, F32),
        compiler_params=pltpu.CompilerParams(
            dimension_semantics=("arbitrary",), vmem_limit_bytes=VMEM_LIMIT),
        name="ffn",
    )(x2d, pre_g, wg, wu, wd, post_g)


def _rope_kernel(pos_ref, freq_ref, cos_ref, sin_ref):
    ang = pos_ref[...] * freq_ref[...]
    cos_ref[...] = jnp.cos(ang)
    sin_ref[...] = jnp.sin(ang)


def _rope_tables(positions):
    b, t = positions.shape
    pos = positions.astype(F32).reshape(b * t, 1)
    inv_freq = ROPE_BASE ** (-jnp.arange(0, DK, 2, dtype=F32) / DK)
    freq = jnp.tile(inv_freq, HEADS).reshape(1, QK_W // 2)
    n = b * t
    out = jax.ShapeDtypeStruct((n, QK_W // 2), F32)
    return pl.pallas_call(
        _rope_kernel,
        grid=(n // ROPE_TILE,),
        in_specs=[
            pl.BlockSpec((ROPE_TILE, 1), lambda i: (i, 0)),
            pl.BlockSpec((1, QK_W // 2), lambda i: (0, 0)),
        ],
        out_specs=[pl.BlockSpec((ROPE_TILE, QK_W // 2), lambda i: (i, 0))] * 2,
        out_shape=[out, out],
        compiler_params=pltpu.CompilerParams(dimension_semantics=("arbitrary",)),
        name="rope_tables",
    )(pos, freq)


def _mixer_kernel(x_ref, cos_ref, sin_ref, pre_g_ref, w_in_ref, w_a2_ref, b_a2_ref,
                  ret_g_ref, gla_g_ref, w_out_ref, post_g_ref,
                  dmat_ref, qdec_ref, kdec_ref, cdec_ref,
                  o_ref,
                  z_ref, la_ref, y_ref, rstate_ref, rstate_bd_ref, gstate_ref):
    tile = x_ref.shape[0]

    @pl.when(pl.program_id(1) == 0)
    def _():
        rstate_ref[...] = jnp.zeros(rstate_ref.shape, F32)
        rstate_bd_ref[...] = jnp.zeros(rstate_bd_ref.shape, BF16)
        gstate_ref[...] = jnp.zeros(gstate_ref.shape, F32)

    x = x_ref[...]
    h = _rms(x, pre_g_ref[...]).astype(BF16)
    z_ref[...] = _dot(h, w_in_ref[...])

    ga = z_ref[:, OFF_GA:OFF_GA + GATE_RANK_PAD].astype(BF16)
    pre = _dot(ga, w_a2_ref[...]) + b_a2_ref[...]
    la_ref[...] = (jnp.minimum(pre, 0.0) - jnp.log1p(jnp.exp(-jnp.abs(pre)))) * (1.0 / GATE_TAU)

    half = QK_W // 2
    hw = DK // 2
    lane_qk = lax.broadcasted_iota(jnp.int32, (1, QK_W), 1)
    ret_masks = [(lane_qk % half) // hw == hd for hd in range(HEADS)]

    ri = lax.broadcasted_iota(jnp.int32, (CHUNK, CHUNK), 0)
    ci = lax.broadcasted_iota(jnp.int32, (CHUNK, CHUNK), 1)
    causal = ri >= ci
    tri_bf = jnp.where(causal, 1.0, 0.0).astype(BF16)
    sub_of_row = lax.broadcasted_iota(jnp.int32, (CHUNK, 1), 0) // GLA_SUB
    lane_v = lax.broadcasted_iota(jnp.int32, (1, V_W), 1)
    dup_half = (lane_v % LANES) // DK
    first_half = lax.broadcasted_iota(jnp.int32, (1, LANES), 1) < DK

    def natural_pair(a, p):
        return jnp.where(first_half, a[:, 2 * p * LANES:(2 * p + 1) * LANES],
                         a[:, (2 * p + 1) * LANES:(2 * p + 2) * LANES])

    def chunk_step(c, carry):
        r0 = pl.multiple_of(c * CHUNK, CHUNK)
        rows = pl.ds(r0, CHUNK)

        cosv = cos_ref[rows, :]
        sinv = sin_ref[rows, :]

        def rope(off):
            x1 = z_ref[rows, off:off + half]
            x2 = z_ref[rows, off + half:off + QK_W]
            return jnp.concatenate([x1 * cosv - x2 * sinv, x1 * sinv + x2 * cosv], axis=-1)

        q = rope(OFF_RQ)
        k = rope(OFF_RK) * QK_SCALE
        v = z_ref[rows, OFF_RV:OFF_RV + V_W].astype(BF16)
        qstack = jnp.concatenate([jnp.where(m, q, 0.0) for m in ret_masks], axis=0).astype(BF16)
        s_all = (_dot_nt(qstack, k.astype(BF16)) * dmat_ref[...]).astype(BF16)
        o_inter = _dot((q * qdec_ref[...]).astype(BF16), rstate_bd_ref[...])
        kdt = (k * kdec_ref[...]).T
        for p in range(HEADS // 2):
            ha, hb = 2 * p, 2 * p + 1
            s_pair = jnp.concatenate([s_all[ha * CHUNK:(ha + 1) * CHUNK],
                                      s_all[hb * CHUNK:(hb + 1) * CHUNK]], axis=-1)
            o_pair = _dot(s_pair, _block_diag2(v[:, ha * DV:(ha + 1) * DV], v[:, hb * DV:(hb + 1) * DV]))
            for hd in (ha, hb):
                cols = slice(hd * DV, (hd + 1) * DV)
                o = o_pair[:, (hd - ha) * DV:(hd - ha + 1) * DV] + o_inter[:, cols]
                o = _rms(o, ret_g_ref[:, cols])
                gate = _silu(z_ref[rows, OFF_RG + hd * DV:OFF_RG + (hd + 1) * DV])
                y_ref[rows, cols] = (o * gate).astype(BF16)
                kd_h = jnp.concatenate([kdt[hd * hw:(hd + 1) * hw],
                                        kdt[half + hd * hw:half + (hd + 1) * hw]], axis=0).astype(BF16)
                st = rstate_ref[hd] * cdec_ref[hd] + _dot(kd_h, v[:, cols])
                rstate_ref[hd] = st
                st_bf = st.astype(BF16)
                rstate_bd_ref[hd * hw:(hd + 1) * hw, cols] = st_bf[:hw]
                rstate_bd_ref[half + hd * hw:half + (hd + 1) * hw, cols] = st_bf[hw:]

        la = la_ref[rows, :]
        a1 = la.astype(BF16)
        a2 = (la - a1.astype(F32)).astype(BF16)
        cum = _dot(tri_bf, a1) + _dot(tri_bf, a2)
        gq = z_ref[rows, OFF_GQ:OFF_GQ + 2 * QK_W] * QK_SCALE
        gk = z_ref[rows, OFF_GK:OFF_GK + 2 * QK_W]
        gv = z_ref[rows, OFF_GV:OFF_GV + V_W].astype(BF16)

        mid = GLA_SUB // 2 - 1
        refs = [cum[s * GLA_SUB + mid:s * GLA_SUB + mid + 1, :] for s in range(N_SUB)]
        last = cum[CHUNK - 1:CHUNK, :]
        ref_of_row = jnp.concatenate(
            [jnp.broadcast_to(r, (GLA_SUB, V_W)) for r in refs], axis=0)
        kt = gk * jnp.exp(ref_of_row - cum)
        q_sets, k_sets = [], []
        for t2 in range(N_SUB // 2):
            sub_of_lane = 2 * t2 + dup_half
            ref_lane = jnp.where(dup_half == 0, refs[2 * t2], refs[2 * t2 + 1])
            q_sets.append((gq * jnp.exp(jnp.where(sub_of_row >= sub_of_lane, cum - ref_lane, 0.0)))
                          .astype(BF16))
            k_sets.append(jnp.where(sub_of_row == sub_of_lane, kt, 0.0).astype(BF16))

        e_cum = jnp.exp(cum)
        e_rest = jnp.exp(last - cum)
        e_last = jnp.exp(last)
        for p in range(HEADS // 2):
            ha, hb = 2 * p, 2 * p + 1
            s_heads = []
            for hd in (ha, hb):
                lt = slice(hd * LANES, (hd + 1) * LANES)
                q_h = jnp.concatenate([qs[:, lt] for qs in q_sets], axis=-1)
                k_h = jnp.concatenate([ks[:, lt] for ks in k_sets], axis=-1)
                s_heads.append(jnp.where(causal, _dot_nt(q_h, k_h), 0.0).astype(BF16))
            o_pair = _dot(jnp.concatenate(s_heads, axis=-1),
                          _block_diag2(gv[:, ha * DV:(ha + 1) * DV], gv[:, hb * DV:(hb + 1) * DV]))
            st = gstate_ref[p]
            qe = natural_pair(gq * e_cum, p).astype(BF16)
            o_pair = o_pair + _dot(qe, st.astype(BF16))
            kdt = natural_pair(gk * e_rest, p).T.astype(BF16)
            dec_col = jnp.broadcast_to(natural_pair(e_last, p), (CHUNK, LANES)).T
            st = st * jnp.concatenate([dec_col, dec_col], axis=-1)
            gstate_ref[p] = st
            for hd in (ha, hb):
                j = hd - ha
                cols = slice(hd * DV, (hd + 1) * DV)
                upd = _dot(kdt[j * DK:(j + 1) * DK], gv[:, cols])
                gstate_ref[p, j * DK:(j + 1) * DK, j * DV:(j + 1) * DV] = (
                    st[j * DK:(j + 1) * DK, j * DV:(j + 1) * DV] + upd)
                o = _rms(o_pair[:, j * DV:(j + 1) * DV], gla_g_ref[:, cols])
                gate = _silu(z_ref[rows, OFF_GG + hd * DV:OFF_GG + (hd + 1) * DV])
                y_ref[rows, V_W + hd * DV:V_W + (hd + 1) * DV] = (o * gate).astype(BF16)
        return carry

    lax.fori_loop(0, tile // CHUNK, chunk_step, 0)

    out = _dot(y_ref[...], w_out_ref[...])
    o_ref[...] = x + _rms(out, post_g_ref[...])


def _mixer(x, layer, cos_t, sin_t, pre_g, w_in, w_a2, b_a2, ret_g, gla_g, w_out, post_g, tables):
    b, t, d = x.shape
    dmat, qdec, kdec, cdec = tables
    const2 = lambda bi, ti: (0, 0)
    lay2 = lambda bi, ti: (layer, 0, 0)
    tok = lambda bi, ti: (bi, ti, 0)
    return pl.pallas_call(
        _mixer_kernel,
        grid=(b, t // MIX_TILE),
        in_specs=[
            pl.BlockSpec((None, MIX_TILE, d), tok),
            pl.BlockSpec((None, MIX_TILE, QK_W // 2), tok),
            pl.BlockSpec((None, MIX_TILE, QK_W // 2), tok),
            pl.BlockSpec((None, 1, d), lay2),
            pl.BlockSpec((None, d, IN_W), lay2),
            pl.BlockSpec((None, GATE_RANK_PAD, 2 * QK_W), lay2),
            pl.BlockSpec((None, 1, 2 * QK_W), lay2),
            pl.BlockSpec((None, 1, V_W), lay2),
            pl.BlockSpec((None, 1, V_W), lay2),
            pl.BlockSpec((None, 2 * V_W, d), lay2),
            pl.BlockSpec((None, 1, d), lay2),
            pl.BlockSpec((HEADS * CHUNK, CHUNK), const2),
            pl.BlockSpec((CHUNK, QK_W), const2),
            pl.BlockSpec((CHUNK, QK_W), const2),
            pl.BlockSpec(memory_space=pltpu.SMEM),
        ],
        out_specs=pl.BlockSpec((None, MIX_TILE, d), tok),
        out_shape=jax.ShapeDtypeStruct(x.shape, F32),
        scratch_shapes=[
            pltpu.VMEM((MIX_TILE, IN_W), F32),
            pltpu.VMEM((MIX_TILE, 2 * QK_W), F32),
            pltpu.VMEM((MIX_TILE, 2 * V_W), BF16),
            pltpu.VMEM((HEADS, DK, DV), F32),
            pltpu.VMEM((QK_W, V_W), BF16),
            pltpu.VMEM((HEADS // 2, 2 * DK, 2 * DV), F32),
        ],
        compiler_params=pltpu.CompilerParams(
            dimension_semantics=("arbitrary", "arbitrary"), vmem_limit_bytes=VMEM_LIMIT),
        name="mixer",
    )(x, cos_t, sin_t, pre_g, w_in, w_a2, b_a2, ret_g, gla_g, w_out, post_g,
      dmat, qdec, kdec, cdec)


def _retention_tables():
    c = CHUNK
    log_gamma = jnp.log1p(-jnp.exp2(-5.0 - jnp.arange(HEADS, dtype=F32)))
    idx = jnp.arange(c, dtype=F32)
    rel = idx[:, None] - idx[None, :]
    causal = rel >= 0
    decay_in = jnp.where(causal[None],
                         jnp.exp(jnp.where(causal, rel, 0.0)[None] * log_gamma[:, None, None]), 0.0)
    q_dec = jnp.exp((idx + 1.0)[None, :] * log_gamma[:, None])
    k_dec = jnp.exp((c - 1.0 - idx)[None, :] * log_gamma[:, None])
    chunk_dec = jnp.exp(c * log_gamma)
    head_of_lane = (np.arange(QK_W) % (QK_W // 2)) // (DK // 2)
    dmat = decay_in.reshape(HEADS * c, c)
    qdec = q_dec.T[:, head_of_lane]
    kdec = k_dec.T[:, head_of_lane]
    return dmat, qdec, kdec, chunk_dec


def _rotary_cols(w):
    lead = w.shape[:-1]
    w = w.reshape(lead + (HEADS, 2, DK // 2))
    return jnp.swapaxes(w, -3, -2).reshape(lead + (QK_W,))


def _dup_cols(w):
    lead = w.shape[:-1]
    w = w.reshape(lead + (HEADS, 1, DK))
    return jnp.broadcast_to(w, lead + (HEADS, 2, DK)).reshape(lead + (2 * QK_W,))


def _relayout_w_in(w_in):
    sizes = (QK_W, QK_W, V_W, V_W, QK_W, QK_W, V_W, GATE_RANK, V_W)
    pts = np.cumsum(sizes)[:-1]
    rq, rk, rv, rg, gq, gk, gv, ga, gg = jnp.split(w_in, pts, axis=-1)
    ga = jnp.pad(ga, ((0, 0), (0, 0), (0, GATE_RANK_PAD - GATE_RANK)))
    return jnp.concatenate([_rotary_cols(rq), _rotary_cols(rk), rv, rg,
                            _dup_cols(gq), _dup_cols(gk), gv, gg, ga], axis=-1)


def _memkv_kernel(mem_ref, g_ref, w_ref, k_ref, v_ref):
    m = _rms(mem_ref[...], g_ref[...]).astype(BF16)
    kv = _dot(m, w_ref[...])
    k_ref[...] = kv[:, :D_MODEL].astype(BF16)
    v_ref[...] = kv[:, D_MODEL:].astype(BF16)


def _memkv(mem, layer, g, w_kv):
    b, m, d = mem.shape
    out = jax.ShapeDtypeStruct((b, m, d), BF16)
    lay2 = lambda i: (layer, 0, 0)
    return pl.pallas_call(
        _memkv_kernel,
        grid=(b,),
        in_specs=[
            pl.BlockSpec((None, m, d), lambda i: (i, 0, 0)),
            pl.BlockSpec((None, 1, d), lay2),
            pl.BlockSpec((None, d, 2 * d), lay2),
        ],
        out_specs=[pl.BlockSpec((None, m, d), lambda i: (i, 0, 0))] * 2,
        out_shape=[out, out],
        compiler_params=pltpu.CompilerParams(
            dimension_semantics=("arbitrary",), vmem_limit_bytes=VMEM_LIMIT),
        name="memkv",
    )(mem, g, w_kv)


def _xattn_kernel(x_ref, k_ref, v_ref, pre_g_ref, wq_ref, wo_ref, post_g_ref, o_ref, att_ref):
    x = x_ref[...]
    h = _rms(x, pre_g_ref[...]).astype(BF16)
    q = _dot(h, wq_ref[...])
    for hd in range(XA_HEADS):
        cols = slice(hd * XA_DH, (hd + 1) * XA_DH)
        s = _dot_nt(q[:, cols].astype(BF16), k_ref[:, cols]) * (XA_DH ** -0.5)
        p = jnp.exp(s - jnp.max(s, axis=-1, keepdims=True))
        o = _dot(p.astype(BF16), v_ref[:, cols]) / jnp.sum(p, axis=-1, keepdims=True)
        att_ref[:, cols] = o.astype(BF16)
    out = _dot(att_ref[...], wo_ref[...])
    o_ref[...] = x + _rms(out, post_g_ref[...])


def _xattn(x, layer, k, v, pre_g, wq, wo, post_g):
    b, t, d = x.shape
    m = k.shape[1]
    lay2 = lambda bi, ti: (layer, 0, 0)
    tok = lambda bi, ti: (bi, ti, 0)
    mem = lambda bi, ti: (bi, 0, 0)
    return pl.pallas_call(
        _xattn_kernel,
        grid=(b, t // XA_TILE),
        in_specs=[
            pl.BlockSpec((None, XA_TILE, d), tok),
            pl.BlockSpec((None, m, d), mem),
            pl.BlockSpec((None, m, d), mem),
            pl.BlockSpec((None, 1, d), lay2),
            pl.BlockSpec((None, d, d), lay2),
            pl.BlockSpec((None, d, d), lay2),
            pl.BlockSpec((None, 1, d), lay2),
        ],
        out_specs=pl.BlockSpec((None, XA_TILE, d), tok),
        out_shape=jax.ShapeDtypeStruct(x.shape, F32),
        scratch_shapes=[pltpu.VMEM((XA_TILE, d), BF16)],
        compiler_params=pltpu.CompilerParams(
            dimension_semantics=("arbitrary", "arbitrary"), vmem_limit_bytes=VMEM_LIMIT),
        name="xattn",
    )(x, k, v, pre_g, wq, wo, post_g)


def kernel(x, mem, positions, ffn1_pre_g, ffn1_w_gate, ffn1_w_up, ffn1_w_down, ffn1_post_g, mix_pre_g, w_in, w_a2, b_a2, ret_norm_g, gla_norm_g, w_out, mix_post_g, xa_pre_g, xa_mem_g, xa_w_q, xa_w_kv, xa_w_o, xa_post_g, ffn2_pre_g, ffn2_w_gate, ffn2_w_up, ffn2_w_down, ffn2_post_g):
    b, t, d = x.shape
    depth = w_in.shape[0]
    assert d == D_MODEL and t % MIX_TILE == 0 and t % XA_TILE == 0 and (b * t) % FFN_TILE == 0
    assert (b * t) % ROPE_TILE == 0

    cos_t, sin_t = _rope_tables(positions)
    cos_t = cos_t.reshape(b, t, QK_W // 2)
    sin_t = sin_t.reshape(b, t, QK_W // 2)
    tables = _retention_tables()

    row = lambda g: g.reshape(depth, 1, -1)
    bf = lambda w: w.astype(BF16)
    ffn1 = (row(ffn1_pre_g), bf(ffn1_w_gate), bf(ffn1_w_up), bf(ffn1_w_down), row(ffn1_post_g))
    ffn2 = (row(ffn2_pre_g), bf(ffn2_w_gate), bf(ffn2_w_up), bf(ffn2_w_down), row(ffn2_post_g))
    w_a2_p = bf(jnp.pad(_dup_cols(w_a2), ((0, 0), (0, GATE_RANK_PAD - GATE_RANK), (0, 0))))
    mix = (row(mix_pre_g), bf(_relayout_w_in(w_in)), w_a2_p, row(_dup_cols(b_a2)),
           row(ret_norm_g), row(gla_norm_g), bf(w_out), row(mix_post_g))
    xa_kv = (row(xa_mem_g), bf(xa_w_kv))
    xa = (row(xa_pre_g), bf(xa_w_q), bf(xa_w_o), row(xa_post_g))

    for l in range(depth):
        x = _ffn(x.reshape(b * t, d), l, *ffn1).reshape(b, t, d)
        x = _mixer(x, l, cos_t, sin_t, *mix, tables)
        k_mem, v_mem = _memkv(mem, l, *xa_kv)
        x = _xattn(x, l, k_mem, v_mem, *xa)
        x = _ffn(x.reshape(b * t, d), l, *ffn2).reshape(b, t, d)
    return x
```

```python
import numpy as np
import jax
import jax.numpy as jnp
from jax import lax
from jax.experimental import pallas as pl
from jax.experimental.pallas import tpu as pltpu

D_MODEL = 1024
D_FF = 2816
EPS = 1e-6
ROPE_BASE = 10000.0

LANES = 128
HEADS = 4
DK = 64
DV = 128
QK_W = HEADS * DK
V_W = HEADS * DV
CHUNK = 128
GLA_SUB = 32
N_SUB = CHUNK // GLA_SUB
GATE_RANK = 16
GATE_RANK_PAD = 128
GATE_TAU = 16.0
QK_SCALE = DK ** -0.5

XA_HEADS = 4
XA_DH = D_MODEL // XA_HEADS

OFF_GA = 0
OFF_GK = OFF_GA + GATE_RANK_PAD
OFF_GQ = OFF_GK + 2 * QK_W
OFF_GV = OFF_GQ + 2 * QK_W
OFF_RK = OFF_GV + V_W
OFF_RQ = OFF_RK + QK_W
OFF_RV = OFF_RQ + QK_W
OFF_GG = OFF_RV + V_W
OFF_RG = OFF_GG + V_W
IN_W = OFF_RG + V_W
LOG2E = 1.4426950408889634

FFN_TILE = 1024
FFN_SUB = 512
FFN_CHUNK = 256
MIX_TILE = 512
XA_TILE = 1024
XA_SUB = 512
ROPE_TILE = 1024
VMEM_LIMIT = 56 * 1024 * 1024

F32 = jnp.float32
BF16 = jnp.bfloat16


def _rms(x, g):
    return x * lax.rsqrt(jnp.mean(x * x, axis=-1, keepdims=True) + EPS) * g


def _silu(x):
    return x * (1.0 / (1.0 + jnp.exp(-x)))


def _dot(a, b):
    return jnp.dot(a, b, preferred_element_type=F32)


def _dot_nt(a, b):
    return lax.dot_general(a, b, (((1,), (1,)), ((), ())), preferred_element_type=F32)


def _block_diag2(a, b):
    za = jnp.zeros(a.shape, a.dtype)
    return jnp.concatenate(
        [jnp.concatenate([a, za], axis=-1), jnp.concatenate([za, b], axis=-1)], axis=0)


def _ffn_kernel(x_ref, pre_g_ref, wg_ref, wu_ref, wd_ref, post_g_ref, o_ref):
    for s in range(FFN_TILE // FFN_SUB):
        rows = slice(s * FFN_SUB, (s + 1) * FFN_SUB)
        x = x_ref[rows, :]
        h = _rms(x, pre_g_ref[...]).astype(BF16)
        acc = jnp.zeros(x.shape, F32)
        for c in range(D_FF // FFN_CHUNK):
            lo = c * FFN_CHUNK
            a = _dot(h, wg_ref[:, lo:lo + FFN_CHUNK])
            b = _dot(h, wu_ref[:, lo:lo + FFN_CHUNK])
            u = (_silu(a) * b).astype(BF16)
            acc = acc + _dot(u, wd_ref[lo:lo + FFN_CHUNK, :])
        o_ref[rows, :] = x + 0.5 * _rms(acc, post_g_ref[...])


def _ffn(x2d, layer, pre_g, wg, wu, wd, post_g):
    n = x2d.shape[0]
    lay2 = lambda i: (layer, 0, 0)
    return pl.pallas_call(
        _ffn_kernel,
        grid=(n // FFN_TILE,),
        in_specs=[
            pl.BlockSpec((FFN_TILE, D_MODEL), lambda i: (i, 0)),
            pl.BlockSpec((None, 1, D_MODEL), lay2),
            pl.BlockSpec((None, D_MODEL, D_FF), lay2, pipeline_mode=pl.Buffered(1)),
            pl.BlockSpec((None, D_MODEL, D_FF), lay2, pipeline_mode=pl.Buffered(1)),
            pl.BlockSpec((None, D_FF, D_MODEL), lay2, pipeline_mode=pl.Buffered(1)),
            pl.BlockSpec((None, 1, D_MODEL), lay2),
        ],
        out_specs=pl.BlockSpec((FFN_TILE, D_MODEL), lambda i: (i, 0)),
        out_shape=jax.ShapeDtypeStruct(x2d.shape, F32),
        compiler_params=pltpu.CompilerParams(
            dimension_semantics=("arbitrary",), vmem_limit_bytes=VMEM_LIMIT),
        name="ffn",
    )(x2d, pre_g, wg, wu, wd, post_g)


def _rope_kernel(pos_ref, freq_ref, cos_ref, sin_ref):
    ang = pos_ref[...] * freq_ref[...]
    cos_ref[...] = jnp.cos(ang)
    sin_ref[...] = jnp.sin(ang)


def _rope_tables(positions):
    b, t = positions.shape
    pos = positions.astype(F32).reshape(b * t, 1)
    inv_freq = ROPE_BASE ** (-jnp.arange(0, DK, 2, dtype=F32) / DK)
    freq = jnp.tile(inv_freq, HEADS).reshape(1, QK_W // 2)
    n = b * t
    out = jax.ShapeDtypeStruct((n, QK_W // 2), F32)
    return pl.pallas_call(
        _rope_kernel,
        grid=(n // ROPE_TILE,),
        in_specs=[
            pl.BlockSpec((ROPE_TILE, 1), lambda i: (i, 0)),
            pl.BlockSpec((1, QK_W // 2), lambda i: (0, 0)),
        ],
        out_specs=[pl.BlockSpec((ROPE_TILE, QK_W // 2), lambda i: (i, 0))] * 2,
        out_shape=[out, out],
        compiler_params=pltpu.CompilerParams(dimension_semantics=("arbitrary",)),
        name="rope_tables",
    )(pos, freq)


def _mixer_kernel(x_ref, cos_ref, sin_ref, pre_g_ref, w_in_ref, w_a2_ref, b_a2_ref,
                  ret_g_ref, gla_g_ref, w_out_ref, post_g_ref,
                  dmat_ref, qdec_ref, kdec_ref, cdec_ref,
                  o_ref,
                  z_ref, la_ref, y_ref, rstate_ref, rstate_bd_ref, gstate_ref):
    tile = x_ref.shape[0]

    @pl.when(pl.program_id(1) == 0)
    def _():
        rstate_ref[...] = jnp.zeros(rstate_ref.shape, F32)
        rstate_bd_ref[...] = jnp.zeros(rstate_bd_ref.shape, BF16)
        gstate_ref[...] = jnp.zeros(gstate_ref.shape, F32)

    x = x_ref[...]
    h = _rms(x, pre_g_ref[...]).astype(BF16)
    z_ref[...] = _dot(h, w_in_ref[...])

    ga = z_ref[:, OFF_GA:OFF_GA + GATE_RANK_PAD].astype(BF16)
    pre = _dot(ga, w_a2_ref[...]) + b_a2_ref[...]
    la_ref[...] = (jnp.minimum(pre, 0.0) - jnp.log1p(jnp.exp(-jnp.abs(pre)))) * (LOG2E / GATE_TAU)

    half = QK_W // 2
    hw = DK // 2
    lane_qk = lax.broadcasted_iota(jnp.int32, (1, QK_W), 1)
    ret_masks = [(lane_qk % half) // hw == hd for hd in range(HEADS)]

    ri = lax.broadcasted_iota(jnp.int32, (CHUNK, CHUNK), 0)
    ci = lax.broadcasted_iota(jnp.int32, (CHUNK, CHUNK), 1)
    causal = ri >= ci
    tri_bf = jnp.where(causal, 1.0, 0.0).astype(BF16)
    lane_v = lax.broadcasted_iota(jnp.int32, (1, V_W), 1)
    dup_half = (lane_v % LANES) // DK
    first_half = lax.broadcasted_iota(jnp.int32, (1, LANES), 1) < DK

    def natural_pair(a, p):
        return jnp.where(first_half, a[:, 2 * p * LANES:(2 * p + 1) * LANES],
                         a[:, (2 * p + 1) * LANES:(2 * p + 2) * LANES])

    def chunk_step(c, carry):
        r0 = pl.multiple_of(c * CHUNK, CHUNK)
        rows = pl.ds(r0, CHUNK)

        cosv = cos_ref[rows, :]
        sinv = sin_ref[rows, :]

        def rope(off):
            x1 = z_ref[rows, off:off + half]
            x2 = z_ref[rows, off + half:off + QK_W]
            return jnp.concatenate([x1 * cosv - x2 * sinv, x1 * sinv + x2 * cosv], axis=-1)

        q = rope(OFF_RQ)
        k = rope(OFF_RK)
        v = z_ref[rows, OFF_RV:OFF_RV + V_W].astype(BF16)
        qstack = jnp.concatenate([jnp.where(m, q, 0.0) for m in ret_masks], axis=0).astype(BF16)
        s_all = (_dot_nt(qstack, k.astype(BF16)) * dmat_ref[...]).astype(BF16)
        o_inter = _dot((q * qdec_ref[...]).astype(BF16), rstate_bd_ref[...])
        kdt = (k * kdec_ref[...]).T
        for p in range(HEADS // 2):
            ha, hb = 2 * p, 2 * p + 1
            s_pair = jnp.concatenate([s_all[ha * CHUNK:(ha + 1) * CHUNK],
                                      s_all[hb * CHUNK:(hb + 1) * CHUNK]], axis=-1)
            o_pair = _dot(s_pair, _block_diag2(v[:, ha * DV:(ha + 1) * DV], v[:, hb * DV:(hb + 1) * DV]))
            for hd in (ha, hb):
                cols = slice(hd * DV, (hd + 1) * DV)
                o = o_pair[:, (hd - ha) * DV:(hd - ha + 1) * DV] + o_inter[:, cols]
                o = _rms(o, ret_g_ref[:, cols])
                gate = _silu(z_ref[rows, OFF_RG + hd * DV:OFF_RG + (hd + 1) * DV])
                y_ref[rows, cols] = (o * gate).astype(BF16)
                kd_h = jnp.concatenate([kdt[hd * hw:(hd + 1) * hw],
                                        kdt[half + hd * hw:half + (hd + 1) * hw]], axis=0).astype(BF16)
                st = rstate_ref[hd] * cdec_ref[hd] + _dot(kd_h, v[:, cols])
                rstate_ref[hd] = st
                st_bf = st.astype(BF16)
                rstate_bd_ref[hd * hw:(hd + 1) * hw, cols] = st_bf[:hw]
                rstate_bd_ref[half + hd * hw:half + (hd + 1) * hw, cols] = st_bf[hw:]

        la = la_ref[rows, :]
        a1 = la.astype(BF16)
        a2 = (la - a1.astype(F32)).astype(BF16)
        cum = _dot(tri_bf, a1) + _dot(tri_bf, a2)
        gq = z_ref[rows, OFF_GQ:OFF_GQ + 2 * QK_W]
        gk = z_ref[rows, OFF_GK:OFF_GK + 2 * QK_W]
        gv = z_ref[rows, OFF_GV:OFF_GV + V_W].astype(BF16)

        mid = GLA_SUB // 2 - 1
        refs = [cum[s * GLA_SUB + mid:s * GLA_SUB + mid + 1, :] for s in range(N_SUB)]
        last = cum[CHUNK - 1:CHUNK, :]
        d = cum - jnp.concatenate([jnp.broadcast_to(r, (GLA_SUB, V_W)) for r in refs], axis=0)
        gq_e = gq * jnp.exp2(d)
        gk_e = gk * jnp.exp2(-d)
        blk = lambda a, rb: a[rb * GLA_SUB:(rb + 1) * GLA_SUB]
        zero_blk = jnp.zeros((GLA_SUB, V_W), BF16)
        q_sets, k_sets = [], []
        for t2 in range(N_SUB // 2):
            q_rows, k_rows = [], []
            for rb in range(N_SUB):
                s_lo, s_hi = 2 * t2, 2 * t2 + 1
                if s_lo > rb:
                    q_rows.append(zero_blk)
                else:
                    f_lo = jnp.exp2(refs[rb] - refs[s_lo])
                    f_hi = jnp.exp2(refs[rb] - refs[s_hi]) if s_hi <= rb else jnp.zeros_like(f_lo)
                    q_rows.append((blk(gq_e, rb) * jnp.where(dup_half == 0, f_lo, f_hi)).astype(BF16))
                if rb // 2 == t2:
                    k_rows.append(jnp.where(dup_half == rb % 2, blk(gk_e, rb), 0.0).astype(BF16))
                else:
                    k_rows.append(zero_blk)
            q_sets.append(jnp.concatenate(q_rows, axis=0))
            k_sets.append(jnp.concatenate(k_rows, axis=0))

        e_last = jnp.exp2(last)
        for p in range(HEADS // 2):
            ha, hb = 2 * p, 2 * p + 1
            s_heads = []
            for hd in (ha, hb):
                lt = slice(hd * LANES, (hd + 1) * LANES)
                q_h = jnp.concatenate([qs[:, lt] for qs in q_sets], axis=-1)
                k_h = jnp.concatenate([ks[:, lt] for ks in k_sets], axis=-1)
                s_heads.append(jnp.where(causal, _dot_nt(q_h, k_h), 0.0).astype(BF16))
            o_pair = _dot(jnp.concatenate(s_heads, axis=-1),
                          _block_diag2(gv[:, ha * DV:(ha + 1) * DV], gv[:, hb * DV:(hb + 1) * DV]))
            st = gstate_ref[p]
            gq_n = natural_pair(gq_e, p)
            gk_n = natural_pair(gk_e, p)
            qe = jnp.concatenate([blk(gq_n, rb) * natural_pair(jnp.exp2(refs[rb]), p)
                                  for rb in range(N_SUB)], axis=0).astype(BF16)
            kd = jnp.concatenate([blk(gk_n, rb) * natural_pair(jnp.exp2(last - refs[rb]), p)
                                  for rb in range(N_SUB)], axis=0)
            o_pair = o_pair + _dot(qe, st.astype(BF16))
            kdt = kd.T.astype(BF16)
            dec_col = jnp.broadcast_to(natural_pair(e_last, p), (CHUNK, LANES)).T
            st = st * jnp.concatenate([dec_col, dec_col], axis=-1)
            gstate_ref[p] = st
            for hd in (ha, hb):
                j = hd - ha
                cols = slice(hd * DV, (hd + 1) * DV)
                upd = _dot(kdt[j * DK:(j + 1) * DK], gv[:, cols])
                gstate_ref[p, j * DK:(j + 1) * DK, j * DV:(j + 1) * DV] = (
                    st[j * DK:(j + 1) * DK, j * DV:(j + 1) * DV] + upd)
                o = _rms(o_pair[:, j * DV:(j + 1) * DV], gla_g_ref[:, cols])
                gate = _silu(z_ref[rows, OFF_GG + hd * DV:OFF_GG + (hd + 1) * DV])
                y_ref[rows, V_W + hd * DV:V_W + (hd + 1) * DV] = (o * gate).astype(BF16)
        return carry

    lax.fori_loop(0, tile // CHUNK, chunk_step, 0, unroll=True)

    out = _dot(y_ref[...], w_out_ref[...])
    o_ref[...] = x + _rms(out, post_g_ref[...])


def _mixer(x, layer, cos_t, sin_t, pre_g, w_in, w_a2, b_a2, ret_g, gla_g, w_out, post_g, tables):
    b, t, d = x.shape
    dmat, qdec, kdec, cdec = tables
    const2 = lambda bi, ti: (0, 0)
    lay2 = lambda bi, ti: (layer, 0, 0)
    tok = lambda bi, ti: (bi, ti, 0)
    return pl.pallas_call(
        _mixer_kernel,
        grid=(b, t // MIX_TILE),
        in_specs=[
            pl.BlockSpec((None, MIX_TILE, d), tok),
            pl.BlockSpec((None, MIX_TILE, QK_W // 2), tok),
            pl.BlockSpec((None, MIX_TILE, QK_W // 2), tok),
            pl.BlockSpec((None, 1, d), lay2),
            pl.BlockSpec((None, d, IN_W), lay2),
            pl.BlockSpec((None, GATE_RANK_PAD, 2 * QK_W), lay2),
            pl.BlockSpec((None, 1, 2 * QK_W), lay2),
            pl.BlockSpec((None, 1, V_W), lay2),
            pl.BlockSpec((None, 1, V_W), lay2),
            pl.BlockSpec((None, 2 * V_W, d), lay2),
            pl.BlockSpec((None, 1, d), lay2),
            pl.BlockSpec((HEADS * CHUNK, CHUNK), const2),
            pl.BlockSpec((CHUNK, QK_W), const2),
            pl.BlockSpec((CHUNK, QK_W), const2),
            pl.BlockSpec(memory_space=pltpu.SMEM),
        ],
        out_specs=pl.BlockSpec((None, MIX_TILE, d), tok),
        out_shape=jax.ShapeDtypeStruct(x.shape, F32),
        scratch_shapes=[
            pltpu.VMEM((MIX_TILE, IN_W), F32),
            pltpu.VMEM((MIX_TILE, 2 * QK_W), F32),
            pltpu.VMEM((MIX_TILE, 2 * V_W), BF16),
            pltpu.VMEM((HEADS, DK, DV), F32),
            pltpu.VMEM((QK_W, V_W), BF16),
            pltpu.VMEM((HEADS // 2, 2 * DK, 2 * DV), F32),
        ],
        compiler_params=pltpu.CompilerParams(
            dimension_semantics=("arbitrary", "arbitrary"), vmem_limit_bytes=VMEM_LIMIT),
        name="mixer",
    )(x, cos_t, sin_t, pre_g, w_in, w_a2, b_a2, ret_g, gla_g, w_out, post_g,
      dmat, qdec, kdec, cdec)


def _retention_tables():
    c = CHUNK
    log_gamma = jnp.log1p(-jnp.exp2(-5.0 - jnp.arange(HEADS, dtype=F32)))
    idx = jnp.arange(c, dtype=F32)
    rel = idx[:, None] - idx[None, :]
    causal = rel >= 0
    decay_in = jnp.where(causal[None],
                         jnp.exp(jnp.where(causal, rel, 0.0)[None] * log_gamma[:, None, None]), 0.0)
    q_dec = jnp.exp((idx + 1.0)[None, :] * log_gamma[:, None])
    k_dec = jnp.exp((c - 1.0 - idx)[None, :] * log_gamma[:, None])
    chunk_dec = jnp.exp(c * log_gamma)
    head_of_lane = (np.arange(QK_W) % (QK_W // 2)) // (DK // 2)
    dmat = decay_in.reshape(HEADS * c, c)
    qdec = q_dec.T[:, head_of_lane]
    kdec = k_dec.T[:, head_of_lane]
    return dmat, qdec, kdec, chunk_dec


def _rotary_cols(w):
    lead = w.shape[:-1]
    w = w.reshape(lead + (HEADS, 2, DK // 2))
    return jnp.swapaxes(w, -3, -2).reshape(lead + (QK_W,))


def _dup_cols(w):
    lead = w.shape[:-1]
    w = w.reshape(lead + (HEADS, 1, DK))
    return jnp.broadcast_to(w, lead + (HEADS, 2, DK)).reshape(lead + (2 * QK_W,))


def _relayout_w_in(w_in):
    sizes = (QK_W, QK_W, V_W, V_W, QK_W, QK_W, V_W, GATE_RANK, V_W)
    pts = np.cumsum(sizes)[:-1]
    rq, rk, rv, rg, gq, gk, gv, ga, gg = jnp.split(w_in, pts, axis=-1)
    ga = jnp.pad(ga, ((0, 0), (0, 0), (0, GATE_RANK_PAD - GATE_RANK)))
    return jnp.concatenate([ga, _dup_cols(gk), _dup_cols(gq) * QK_SCALE, gv,
                            _rotary_cols(rk) * QK_SCALE, _rotary_cols(rq), rv, gg, rg], axis=-1)


def _memkv_kernel(mem_ref, g_ref, w_ref, k_ref, v_ref):
    m = _rms(mem_ref[...], g_ref[...]).astype(BF16)
    kv = _dot(m, w_ref[...])
    k_ref[...] = kv[:, :D_MODEL].astype(BF16)
    v_ref[...] = kv[:, D_MODEL:].astype(BF16)


def _memkv(mem, layer, g, w_kv):
    b, m, d = mem.shape
    out = jax.ShapeDtypeStruct((b, m, d), BF16)
    lay2 = lambda i: (layer, 0, 0)
    return pl.pallas_call(
        _memkv_kernel,
        grid=(b,),
        in_specs=[
            pl.BlockSpec((None, m, d), lambda i: (i, 0, 0)),
            pl.BlockSpec((None, 1, d), lay2),
            pl.BlockSpec((None, d, 2 * d), lay2),
        ],
        out_specs=[pl.BlockSpec((None, m, d), lambda i: (i, 0, 0))] * 2,
        out_shape=[out, out],
        compiler_params=pltpu.CompilerParams(
            dimension_semantics=("arbitrary",), vmem_limit_bytes=VMEM_LIMIT),
        name="memkv",
    )(mem, g, w_kv)


def _xattn_kernel(x_ref, k_ref, v_ref, pre_g_ref, wq_ref, wo_ref, post_g_ref, o_ref, att_ref):
    for sub in range(XA_TILE // XA_SUB):
        rows = slice(sub * XA_SUB, (sub + 1) * XA_SUB)
        x = x_ref[rows, :]
        h = _rms(x, pre_g_ref[...]).astype(BF16)
        q = _dot(h, wq_ref[...])
        for hd in range(XA_HEADS):
            cols = slice(hd * XA_DH, (hd + 1) * XA_DH)
            s = _dot_nt(q[:, cols].astype(BF16), k_ref[:, cols]) * (XA_DH ** -0.5)
            p = jnp.exp(s - jnp.max(s, axis=-1, keepdims=True))
            o = _dot(p.astype(BF16), v_ref[:, cols]) / jnp.sum(p, axis=-1, keepdims=True)
            att_ref[rows, cols] = o.astype(BF16)
        out = _dot(att_ref[rows, :], wo_ref[...])
        o_ref[rows, :] = x + _rms(out, post_g_ref[...])


def _xattn(x, layer, k, v, pre_g, wq, wo, post_g):
    b, t, d = x.shape
    m = k.shape[1]
    lay2 = lambda bi, ti: (layer, 0, 0)
    tok = lambda bi, ti: (bi, ti, 0)
    mem = lambda bi, ti: (bi, 0, 0)
    return pl.pallas_call(
        _xattn_kernel,
        grid=(b, t // XA_TILE),
        in_specs=[
            pl.BlockSpec((None, XA_TILE, d), tok),
            pl.BlockSpec((None, m, d), mem),
            pl.BlockSpec((None, m, d), mem),
            pl.BlockSpec((None, 1, d), lay2),
            pl.BlockSpec((None, d, d), lay2),
            pl.BlockSpec((None, d, d), lay2),
            pl.BlockSpec((None, 1, d), lay2),
        ],
        out_specs=pl.BlockSpec((None, XA_TILE, d), tok),
        out_shape=jax.ShapeDtypeStruct(x.shape, F32),
        scratch_shapes=[pltpu.VMEM((XA_TILE, d), BF16)],
        compiler_params=pltpu.CompilerParams(
            dimension_semantics=("arbitrary", "arbitrary"), vmem_limit_bytes=VMEM_LIMIT),
        name="xattn",
    )(x, k, v, pre_g, wq, wo, post_g)


def kernel(x, mem, positions, ffn1_pre_g, ffn1_w_gate, ffn1_w_up, ffn1_w_down, ffn1_post_g, mix_pre_g, w_in, w_a2, b_a2, ret_norm_g, gla_norm_g, w_out, mix_post_g, xa_pre_g, xa_mem_g, xa_w_q, xa_w_kv, xa_w_o, xa_post_g, ffn2_pre_g, ffn2_w_gate, ffn2_w_up, ffn2_w_down, ffn2_post_g):
    b, t, d = x.shape
    depth = w_in.shape[0]
    assert d == D_MODEL and t % MIX_TILE == 0 and t % XA_TILE == 0 and (b * t) % FFN_TILE == 0
    assert (b * t) % ROPE_TILE == 0

    cos_t, sin_t = _rope_tables(positions)
    cos_t = cos_t.reshape(b, t, QK_W // 2)
    sin_t = sin_t.reshape(b, t, QK_W // 2)
    tables = _retention_tables()

    row = lambda g: g.reshape(depth, 1, -1)
    bf = lambda w: w.astype(BF16)
    ffn1 = (row(ffn1_pre_g), bf(ffn1_w_gate), bf(ffn1_w_up), bf(ffn1_w_down), row(ffn1_post_g))
    ffn2 = (row(ffn2_pre_g), bf(ffn2_w_gate), bf(ffn2_w_up), bf(ffn2_w_down), row(ffn2_post_g))
    w_a2_p = bf(jnp.pad(_dup_cols(w_a2), ((0, 0), (0, GATE_RANK_PAD - GATE_RANK), (0, 0))))
    mix = (row(mix_pre_g), bf(_relayout_w_in(w_in)), w_a2_p, row(_dup_cols(b_a2)),
           row(ret_norm_g), row(gla_norm_g), bf(w_out), row(mix_post_g))
    xa_kv = (row(xa_mem_g), bf(xa_w_kv))
    xa = (row(xa_pre_g), bf(xa_w_q), bf(xa_w_o), row(xa_post_g))

    for l in range(depth):
        x = _ffn(x.reshape(b * t, d), l, *ffn1).reshape(b, t, d)
        x = _mixer(x, l, cos_t, sin_t, *mix, tables)
        k_mem, v_mem = _memkv(mem, l, *xa_kv)
        x = _xattn(x, l, k_mem, v_mem, *xa)
        x = _ffn(x.reshape(b * t, d), l, *ffn2).reshape(b, t, d)
    return x
```

```python
import numpy as np
import jax
import jax.numpy as jnp
from jax import lax
from jax.experimental import pallas as pl
from jax.experimental.pallas import tpu as pltpu

D_MODEL = 1024
D_FF = 2816
EPS = 1e-6
ROPE_BASE = 10000.0

LANES = 128
HEADS = 4
DK = 64
DV = 128
QK_W = HEADS * DK
V_W = HEADS * DV
CHUNK = 128
GLA_SUB = 32
N_SUB = CHUNK // GLA_SUB
GATE_RANK = 16
GATE_RANK_PAD = 128
GATE_TAU = 16.0
QK_SCALE = DK ** -0.5

XA_HEADS = 4
XA_DH = D_MODEL // XA_HEADS

OFF_GA = 0
OFF_GK = OFF_GA + GATE_RANK_PAD
OFF_GQ = OFF_GK + 2 * QK_W
OFF_GV = OFF_GQ + 2 * QK_W
OFF_RK = OFF_GV + V_W
OFF_RQ = OFF_RK + QK_W
OFF_RV = OFF_RQ + QK_W
OFF_GG = OFF_RV + V_W
OFF_RG = OFF_GG + V_W
IN_W = OFF_RG + V_W
N_GROUPS = 4
IN_GROUPS = (0, OFF_GV, OFF_RV, OFF_RG, IN_W)
OUT_GROUP = D_MODEL // N_GROUPS
LOG2E = 1.4426950408889634

FFN_TILE = 1024
FFN_SUB = 512
FFN_CHUNK = 256
MIX_TILE = 1024
MIX_SUB = 512
XA_TILE = 1024
XA_SUB = 512
ROPE_TILE = 1024
VMEM_LIMIT = 56 * 1024 * 1024

F32 = jnp.float32
BF16 = jnp.bfloat16


def _rms(x, g):
    return x * lax.rsqrt(jnp.mean(x * x, axis=-1, keepdims=True) + EPS) * g


def _silu(x):
    return x * (1.0 / (1.0 + jnp.exp(-x)))


def _dot(a, b):
    return jnp.dot(a, b, preferred_element_type=F32)


def _dot_nt(a, b):
    return lax.dot_general(a, b, (((1,), (1,)), ((), ())), preferred_element_type=F32)


def _block_diag2(a, b):
    za = jnp.zeros(a.shape, a.dtype)
    return jnp.concatenate(
        [jnp.concatenate([a, za], axis=-1), jnp.concatenate([za, b], axis=-1)], axis=0)


def _ffn_kernel(x_ref, pre_g_ref, wg_ref, wu_ref, wd_ref, post_g_ref, o_ref):
    for s in range(FFN_TILE // FFN_SUB):
        rows = slice(s * FFN_SUB, (s + 1) * FFN_SUB)
        x = x_ref[rows, :]
        h = _rms(x, pre_g_ref[...]).astype(BF16)
        acc = jnp.zeros(x.shape, F32)
        for c in range(D_FF // FFN_CHUNK):
            lo = c * FFN_CHUNK
            a = _dot(h, wg_ref[:, lo:lo + FFN_CHUNK])
            b = _dot(h, wu_ref[:, lo:lo + FFN_CHUNK])
            u = (_silu(a) * b).astype(BF16)
            acc = acc + _dot(u, wd_ref[lo:lo + FFN_CHUNK, :])
        o_ref[rows, :] = x + 0.5 * _rms(acc, post_g_ref[...])


def _ffn(x2d, layer, pre_g, wg, wu, wd, post_g):
    n = x2d.shape[0]
    lay2 = lambda i: (layer, 0, 0)
    return pl.pallas_call(
        _ffn_kernel,
        grid=(n // FFN_TILE,),
        in_specs=[
            pl.BlockSpec((FFN_TILE, D_MODEL), lambda i: (i, 0)),
            pl.BlockSpec((None, 1, D_MODEL), lay2),
            pl.BlockSpec((None, D_MODEL, D_FF), lay2, pipeline_mode=pl.Buffered(1)),
            pl.BlockSpec((None, D_MODEL, D_FF), lay2, pipeline_mode=pl.Buffered(1)),
            pl.BlockSpec((None, D_FF, D_MODEL), lay2, pipeline_mode=pl.Buffered(1)),
            pl.BlockSpec((None, 1, D_MODEL), lay2),
        ],
        out_specs=pl.BlockSpec((FFN_TILE, D_MODEL), lambda i: (i, 0)),
        out_shape=jax.ShapeDtypeStruct(x2d.shape, F32),
        compiler_params=pltpu.CompilerParams(
            dimension_semantics=("arbitrary",), vmem_limit_bytes=VMEM_LIMIT),
        name="ffn",
    )(x2d, pre_g, wg, wu, wd, post_g)


def _rope_kernel(pos_ref, freq_ref, cos_ref, sin_ref):
    ang = pos_ref[...] * freq_ref[...]
    cos_ref[...] = jnp.cos(ang)
    sin_ref[...] = jnp.sin(ang)


def _rope_tables(positions):
    b, t = positions.shape
    pos = positions.astype(F32).reshape(b * t, 1)
    inv_freq = ROPE_BASE ** (-jnp.arange(0, DK, 2, dtype=F32) / DK)
    freq = jnp.tile(inv_freq, HEADS).reshape(1, QK_W // 2)
    n = b * t
    out = jax.ShapeDtypeStruct((n, QK_W // 2), F32)
    return pl.pallas_call(
        _rope_kernel,
        grid=(n // ROPE_TILE,),
        in_specs=[
            pl.BlockSpec((ROPE_TILE, 1), lambda i: (i, 0)),
            pl.BlockSpec((1, QK_W // 2), lambda i: (0, 0)),
        ],
        out_specs=[pl.BlockSpec((ROPE_TILE, QK_W // 2), lambda i: (i, 0))] * 2,
        out_shape=[out, out],
        compiler_params=pltpu.CompilerParams(dimension_semantics=("arbitrary",)),
        name="rope_tables",
    )(pos, freq)


def _mixer_kernel(x_ref, cos_ref, sin_ref, pre_g_ref, w_in_ref, w_a2_ref, b_a2_ref,
                  ret_g_ref, gla_g_ref, w_out_ref, post_g_ref,
                  dmat_ref, qdec_ref, kdec_ref, cdec_ref,
                  o_ref,
                  h_ref, z_ref, la_ref, y_ref, rstate_ref, rstate_bd_ref, gstate_ref):
    tile = x_ref.shape[0]

    @pl.when(pl.program_id(1) == 0)
    def _():
        rstate_ref[...] = jnp.zeros(rstate_ref.shape, F32)
        rstate_bd_ref[...] = jnp.zeros(rstate_bd_ref.shape, BF16)
        gstate_ref[...] = jnp.zeros(gstate_ref.shape, F32)

    def pre_norm(srows):
        h_ref[srows, :] = _rms(x_ref[srows, :], pre_g_ref[...]).astype(BF16)

    def in_projection(srows, g):
        lo, hi = IN_GROUPS[g], IN_GROUPS[g + 1]
        z_ref[srows, lo:hi] = _dot(h_ref[srows, :], w_in_ref[:, lo:hi])
        if g == 0:
            ga = z_ref[srows, OFF_GA:OFF_GA + GATE_RANK_PAD].astype(BF16)
            pre = _dot(ga, w_a2_ref[...]) + b_a2_ref[...]
            la_ref[srows, :] = ((jnp.minimum(pre, 0.0) - jnp.log1p(jnp.exp(-jnp.abs(pre))))
                                * (LOG2E / GATE_TAU))

    def out_projection(srows, g):
        cols = slice(g * OUT_GROUP, (g + 1) * OUT_GROUP)
        o_ref[srows, cols] = _dot(y_ref[srows, :], w_out_ref[:, cols])

    def post_norm(srows):
        o_ref[srows, :] = x_ref[srows, :] + _rms(o_ref[srows, :], post_g_ref[...])

    half = QK_W // 2
    hw = DK // 2
    lane_qk = lax.broadcasted_iota(jnp.int32, (1, QK_W), 1)
    ret_masks = [(lane_qk % half) // hw == hd for hd in range(HEADS)]

    ri = lax.broadcasted_iota(jnp.int32, (CHUNK, CHUNK), 0)
    ci = lax.broadcasted_iota(jnp.int32, (CHUNK, CHUNK), 1)
    causal = ri >= ci
    tri_bf = jnp.where(causal, 1.0, 0.0).astype(BF16)
    lane_v = lax.broadcasted_iota(jnp.int32, (1, V_W), 1)
    dup_half = (lane_v % LANES) // DK
    first_half = lax.broadcasted_iota(jnp.int32, (1, LANES), 1) < DK

    def natural_pair(a, p):
        return jnp.where(first_half, a[:, 2 * p * LANES:(2 * p + 1) * LANES],
                         a[:, (2 * p + 1) * LANES:(2 * p + 2) * LANES])

    blk = lambda a, rb: a[rb * GLA_SUB:(rb + 1) * GLA_SUB]

    def stage_a(c):
        rows = slice(c * CHUNK, (c + 1) * CHUNK)
        out = {}

        cosv = cos_ref[rows, :]
        sinv = sin_ref[rows, :]

        def rope(off):
            x1 = z_ref[rows, off:off + half]
            x2 = z_ref[rows, off + half:off + QK_W]
            return jnp.concatenate([x1 * cosv - x2 * sinv, x1 * sinv + x2 * cosv], axis=-1)

        q = rope(OFF_RQ)
        k = rope(OFF_RK)
        v = z_ref[rows, OFF_RV:OFF_RV + V_W].astype(BF16)
        qstack = jnp.concatenate([jnp.where(m, q, 0.0) for m in ret_masks], axis=0).astype(BF16)
        s_all = (_dot_nt(qstack, k.astype(BF16)) * dmat_ref[...]).astype(BF16)
        kdt = (k * kdec_ref[...]).T
        out["r_v"] = v
        out["r_s"] = s_all
        out["r_qd"] = (q * qdec_ref[...]).astype(BF16)
        out["r_kd"] = [jnp.concatenate([kdt[hd * hw:(hd + 1) * hw],
                                        kdt[half + hd * hw:half + (hd + 1) * hw]], axis=0).astype(BF16)
                       for hd in range(HEADS)]

        la = la_ref[rows, :]
        a1 = la.astype(BF16)
        a2 = (la - a1.astype(F32)).astype(BF16)
        cum = _dot(tri_bf, a1) + _dot(tri_bf, a2)
        gq = z_ref[rows, OFF_GQ:OFF_GQ + 2 * QK_W]
        gk = z_ref[rows, OFF_GK:OFF_GK + 2 * QK_W]
        gv = z_ref[rows, OFF_GV:OFF_GV + V_W].astype(BF16)

        mid = GLA_SUB // 2 - 1
        refs = [cum[s * GLA_SUB + mid:s * GLA_SUB + mid + 1, :] for s in range(N_SUB)]
        last = cum[CHUNK - 1:CHUNK, :]
        d = cum - jnp.concatenate([jnp.broadcast_to(r, (GLA_SUB, V_W)) for r in refs], axis=0)
        gq_e = gq * jnp.exp2(d)
        gk_e = gk * jnp.exp2(-d)
        zero_blk = jnp.zeros((GLA_SUB, V_W), BF16)
        q_sets, k_sets = [], []
        for t2 in range(N_SUB // 2):
            q_rows, k_rows = [], []
            for rb in range(N_SUB):
                s_lo, s_hi = 2 * t2, 2 * t2 + 1
                if s_lo > rb:
                    q_rows.append(zero_blk)
                else:
                    f_lo = jnp.exp2(refs[rb] - refs[s_lo])
                    f_hi = jnp.exp2(refs[rb] - refs[s_hi]) if s_hi <= rb else jnp.zeros_like(f_lo)
                    q_rows.append((blk(gq_e, rb) * jnp.where(dup_half == 0, f_lo, f_hi)).astype(BF16))
                if rb // 2 == t2:
                    k_rows.append(jnp.where(dup_half == rb % 2, blk(gk_e, rb), 0.0).astype(BF16))
                else:
                    k_rows.append(zero_blk)
            q_sets.append(jnp.concatenate(q_rows, axis=0))
            k_sets.append(jnp.concatenate(k_rows, axis=0))

        e_last = jnp.exp2(last)
        out["g_v"] = gv
        out["g_q"] = [jnp.concatenate([qs[:, hd * LANES:(hd + 1) * LANES] for qs in q_sets], axis=-1)
                      for hd in range(HEADS)]
        out["g_k"] = [jnp.concatenate([ks[:, hd * LANES:(hd + 1) * LANES] for ks in k_sets], axis=-1)
                      for hd in range(HEADS)]
        out["g_qe"], out["g_kdt"], out["g_dec"] = [], [], []
        for p in range(HEADS // 2):
            gq_n = natural_pair(gq_e, p)
            gk_n = natural_pair(gk_e, p)
            out["g_qe"].append(jnp.concatenate(
                [blk(gq_n, rb) * natural_pair(jnp.exp2(refs[rb]), p) for rb in range(N_SUB)],
                axis=0).astype(BF16))
            kd = jnp.concatenate(
                [blk(gk_n, rb) * natural_pair(jnp.exp2(last - refs[rb]), p) for rb in range(N_SUB)],
                axis=0)
            out["g_kdt"].append(kd.T.astype(BF16))
            out["g_dec"].append(jnp.broadcast_to(natural_pair(e_last, p), (CHUNK, LANES)).T)
        return out

    def stage_b(c, a):
        rows = slice(c * CHUNK, (c + 1) * CHUNK)
        g_s = [jnp.where(causal, _dot_nt(a["g_q"][hd], a["g_k"][hd]), 0.0).astype(BF16)
               for hd in range(HEADS)]

        v, s_all = a["r_v"], a["r_s"]
        o_inter = _dot(a["r_qd"], rstate_bd_ref[...])
        for p in range(HEADS // 2):
            ha, hb = 2 * p, 2 * p + 1
            s_pair = jnp.concatenate([s_all[ha * CHUNK:(ha + 1) * CHUNK],
                                      s_all[hb * CHUNK:(hb + 1) * CHUNK]], axis=-1)
            o_pair = _dot(s_pair, _block_diag2(v[:, ha * DV:(ha + 1) * DV], v[:, hb * DV:(hb + 1) * DV]))
            for hd in (ha, hb):
                cols = slice(hd * DV, (hd + 1) * DV)
                o = o_pair[:, (hd - ha) * DV:(hd - ha + 1) * DV] + o_inter[:, cols]
                o = _rms(o, ret_g_ref[:, cols])
                gate = _silu(z_ref[rows, OFF_RG + hd * DV:OFF_RG + (hd + 1) * DV])
                y_ref[rows, cols] = (o * gate).astype(BF16)
                st = rstate_ref[hd] * cdec_ref[hd] + _dot(a["r_kd"][hd], v[:, cols])
                rstate_ref[hd] = st
                st_bf = st.astype(BF16)
                rstate_bd_ref[hd * hw:(hd + 1) * hw, cols] = st_bf[:hw]
                rstate_bd_ref[half + hd * hw:half + (hd + 1) * hw, cols] = st_bf[hw:]

        gv = a["g_v"]
        for p in range(HEADS // 2):
            ha, hb = 2 * p, 2 * p + 1
            st = gstate_ref[p]
            o_pair = (_dot(a["g_qe"][p], st.astype(BF16))
                      + _dot(jnp.concatenate([g_s[ha], g_s[hb]], axis=-1),
                             _block_diag2(gv[:, ha * DV:(ha + 1) * DV], gv[:, hb * DV:(hb + 1) * DV])))
            dec_col = a["g_dec"][p]
            st = st * jnp.concatenate([dec_col, dec_col], axis=-1)
            gstate_ref[p] = st
            kdt = a["g_kdt"][p]
            for hd in (ha, hb):
                j = hd - ha
                cols = slice(hd * DV, (hd + 1) * DV)
                upd = _dot(kdt[j * DK:(j + 1) * DK], gv[:, cols])
                gstate_ref[p, j * DK:(j + 1) * DK, j * DV:(j + 1) * DV] = (
                    st[j * DK:(j + 1) * DK, j * DV:(j + 1) * DV] + upd)
                o = _rms(o_pair[:, j * DV:(j + 1) * DV], gla_g_ref[:, cols])
                gate = _silu(z_ref[rows, OFF_GG + hd * DV:OFF_GG + (hd + 1) * DV])
                y_ref[rows, V_W + hd * DV:V_W + (hd + 1) * DV] = (o * gate).astype(BF16)

    n_sub_chunks = MIX_SUB // CHUNK
    assert tile == 2 * MIX_SUB and n_sub_chunks == N_GROUPS
    rows_a, rows_b = slice(0, MIX_SUB), slice(MIX_SUB, tile)
    pre_norm(rows_a)
    for g in range(N_GROUPS):
        in_projection(rows_a, g)
    pre_norm(rows_b)
    n_chunks = 2 * n_sub_chunks
    fillers = {0: [(in_projection, rows_b, 0)],
               1: [(in_projection, rows_b, 1), (in_projection, rows_b, 2)],
               2: [(in_projection, rows_b, 3)],
               4: [(out_projection, rows_a, 0)],
               5: [(out_projection, rows_a, 1)],
               6: [(out_projection, rows_a, 2)],
               7: [(out_projection, rows_a, 3)]}
    ahead = stage_a(0)
    for c in range(n_chunks):
        cur = ahead
        if c + 1 < n_chunks:
            ahead = stage_a(c + 1)
        stage_b(c, cur)
        for fn, srows, g in fillers.get(c, []):
            fn(srows, g)
    post_norm(rows_a)
    for g in range(N_GROUPS):
        out_projection(rows_b, g)
    post_norm(rows_b)


def _mixer(x, layer, cos_t, sin_t, pre_g, w_in, w_a2, b_a2, ret_g, gla_g, w_out, post_g, tables):
    b, t, d = x.shape
    dmat, qdec, kdec, cdec = tables
    const2 = lambda bi, ti: (0, 0)
    lay2 = lambda bi, ti: (layer, 0, 0)
    tok = lambda bi, ti: (bi, ti, 0)
    return pl.pallas_call(
        _mixer_kernel,
        grid=(b, t // MIX_TILE),
        in_specs=[
            pl.BlockSpec((None, MIX_TILE, d), tok),
            pl.BlockSpec((None, MIX_TILE, QK_W // 2), tok),
            pl.BlockSpec((None, MIX_TILE, QK_W // 2), tok),
            pl.BlockSpec((None, 1, d), lay2),
            pl.BlockSpec((None, d, IN_W), lay2, pipeline_mode=pl.Buffered(1)),
            pl.BlockSpec((None, GATE_RANK_PAD, 2 * QK_W), lay2),
            pl.BlockSpec((None, 1, 2 * QK_W), lay2),
            pl.BlockSpec((None, 1, V_W), lay2),
            pl.BlockSpec((None, 1, V_W), lay2),
            pl.BlockSpec((None, 2 * V_W, d), lay2, pipeline_mode=pl.Buffered(1)),
            pl.BlockSpec((None, 1, d), lay2),
            pl.BlockSpec((HEADS * CHUNK, CHUNK), const2),
            pl.BlockSpec((CHUNK, QK_W), const2),
            pl.BlockSpec((CHUNK, QK_W), const2),
            pl.BlockSpec(memory_space=pltpu.SMEM),
        ],
        out_specs=pl.BlockSpec((None, MIX_TILE, d), tok),
        out_shape=jax.ShapeDtypeStruct(x.shape, F32),
        scratch_shapes=[
            pltpu.VMEM((MIX_TILE, d), BF16),
            pltpu.VMEM((MIX_TILE, IN_W), F32),
            pltpu.VMEM((MIX_TILE, 2 * QK_W), F32),
            pltpu.VMEM((MIX_TILE, 2 * V_W), BF16),
            pltpu.VMEM((HEADS, DK, DV), F32),
            pltpu.VMEM((QK_W, V_W), BF16),
            pltpu.VMEM((HEADS // 2, 2 * DK, 2 * DV), F32),
        ],
        compiler_params=pltpu.CompilerParams(
            dimension_semantics=("arbitrary", "arbitrary"), vmem_limit_bytes=VMEM_LIMIT),
        name="mixer",
    )(x, cos_t, sin_t, pre_g, w_in, w_a2, b_a2, ret_g, gla_g, w_out, post_g,
      dmat, qdec, kdec, cdec)


def _retention_tables():
    c = CHUNK
    log_gamma = jnp.log1p(-jnp.exp2(-5.0 - jnp.arange(HEADS, dtype=F32)))
    idx = jnp.arange(c, dtype=F32)
    rel = idx[:, None] - idx[None, :]
    causal = rel >= 0
    decay_in = jnp.where(causal[None],
                         jnp.exp(jnp.where(causal, rel, 0.0)[None] * log_gamma[:, None, None]), 0.0)
    q_dec = jnp.exp((idx + 1.0)[None, :] * log_gamma[:, None])
    k_dec = jnp.exp((c - 1.0 - idx)[None, :] * log_gamma[:, None])
    chunk_dec = jnp.exp(c * log_gamma)
    head_of_lane = (np.arange(QK_W) % (QK_W // 2)) // (DK // 2)
    dmat = decay_in.reshape(HEADS * c, c)
    qdec = q_dec.T[:, head_of_lane]
    kdec = k_dec.T[:, head_of_lane]
    return dmat, qdec, kdec, chunk_dec


def _rotary_cols(w):
    lead = w.shape[:-1]
    w = w.reshape(lead + (HEADS, 2, DK // 2))
    return jnp.swapaxes(w, -3, -2).reshape(lead + (QK_W,))


def _dup_cols(w):
    lead = w.shape[:-1]
    w = w.reshape(lead + (HEADS, 1, DK))
    return jnp.broadcast_to(w, lead + (HEADS, 2, DK)).reshape(lead + (2 * QK_W,))


def _relayout_w_in(w_in):
    sizes = (QK_W, QK_W, V_W, V_W, QK_W, QK_W, V_W, GATE_RANK, V_W)
    pts = np.cumsum(sizes)[:-1]
    rq, rk, rv, rg, gq, gk, gv, ga, gg = jnp.split(w_in, pts, axis=-1)
    ga = jnp.pad(ga, ((0, 0), (0, 0), (0, GATE_RANK_PAD - GATE_RANK)))
    return jnp.concatenate([ga, _dup_cols(gk), _dup_cols(gq) * QK_SCALE, gv,
                            _rotary_cols(rk) * QK_SCALE, _rotary_cols(rq), rv, gg, rg], axis=-1)


def _memkv_kernel(mem_ref, g_ref, w_ref, k_ref, v_ref):
    m = _rms(mem_ref[...], g_ref[...]).astype(BF16)
    kv = _dot(m, w_ref[...])
    k_ref[...] = kv[:, :D_MODEL].T.astype(BF16)
    v_ref[...] = kv[:, D_MODEL:].astype(BF16)


def _memkv(mem, layer, g, w_kv):
    b, m, d = mem.shape
    lay2 = lambda i: (layer, 0, 0)
    return pl.pallas_call(
        _memkv_kernel,
        grid=(b,),
        in_specs=[
            pl.BlockSpec((None, m, d), lambda i: (i, 0, 0)),
            pl.BlockSpec((None, 1, d), lay2),
            pl.BlockSpec((None, d, 2 * d), lay2),
        ],
        out_specs=[pl.BlockSpec((None, d, m), lambda i: (i, 0, 0)),
                   pl.BlockSpec((None, m, d), lambda i: (i, 0, 0))],
        out_shape=[jax.ShapeDtypeStruct((b, d, m), BF16), jax.ShapeDtypeStruct((b, m, d), BF16)],
        compiler_params=pltpu.CompilerParams(
            dimension_semantics=("arbitrary",), vmem_limit_bytes=VMEM_LIMIT),
        name="memkv",
    )(mem, g, w_kv)


def _xattn_kernel(x_ref, kt_ref, v_ref, pre_g_ref, wq_ref, wo_ref, post_g_ref, o_ref, att_ref):
    for sub in range(XA_TILE // XA_SUB):
        rows = slice(sub * XA_SUB, (sub + 1) * XA_SUB)
        x = x_ref[rows, :]
        h = _rms(x, pre_g_ref[...]).astype(BF16)
        q = _dot(h, wq_ref[...])
        for hd in range(XA_HEADS):
            cols = slice(hd * XA_DH, (hd + 1) * XA_DH)
            s = _dot(q[:, cols].astype(BF16), kt_ref[cols, :]) * (XA_DH ** -0.5)
            p = jnp.exp(s - jnp.max(s, axis=-1, keepdims=True))
            o = _dot(p.astype(BF16), v_ref[:, cols]) / jnp.sum(p, axis=-1, keepdims=True)
            att_ref[rows, cols] = o.astype(BF16)
        out = _dot(att_ref[rows, :], wo_ref[...])
        o_ref[rows, :] = x + _rms(out, post_g_ref[...])


def _xattn(x, layer, kt, v, pre_g, wq, wo, post_g):
    b, t, d = x.shape
    m = v.shape[1]
    lay2 = lambda bi, ti: (layer, 0, 0)
    tok = lambda bi, ti: (bi, ti, 0)
    mem = lambda bi, ti: (bi, 0, 0)
    return pl.pallas_call(
        _xattn_kernel,
        grid=(b, t // XA_TILE),
        in_specs=[
            pl.BlockSpec((None, XA_TILE, d), tok),
            pl.BlockSpec((None, d, m), mem),
            pl.BlockSpec((None, m, d), mem),
            pl.BlockSpec((None, 1, d), lay2),
            pl.BlockSpec((None, d, d), lay2),
            pl.BlockSpec((None, d, d), lay2),
            pl.BlockSpec((None, 1, d), lay2),
        ],
        out_specs=pl.BlockSpec((None, XA_TILE, d), tok),
        out_shape=jax.ShapeDtypeStruct(x.shape, F32),
        scratch_shapes=[pltpu.VMEM((XA_TILE, d), BF16)],
        compiler_params=pltpu.CompilerParams(
            dimension_semantics=("arbitrary", "arbitrary"), vmem_limit_bytes=VMEM_LIMIT),
        name="xattn",
    )(x, kt, v, pre_g, wq, wo, post_g)


def kernel(x, mem, positions, ffn1_pre_g, ffn1_w_gate, ffn1_w_up, ffn1_w_down, ffn1_post_g, mix_pre_g, w_in, w_a2, b_a2, ret_norm_g, gla_norm_g, w_out, mix_post_g, xa_pre_g, xa_mem_g, xa_w_q, xa_w_kv, xa_w_o, xa_post_g, ffn2_pre_g, ffn2_w_gate, ffn2_w_up, ffn2_w_down, ffn2_post_g):
    b, t, d = x.shape
    depth = w_in.shape[0]
    assert d == D_MODEL and t % MIX_TILE == 0 and t % XA_TILE == 0 and (b * t) % FFN_TILE == 0
    assert (b * t) % ROPE_TILE == 0

    cos_t, sin_t = _rope_tables(positions)
    cos_t = cos_t.reshape(b, t, QK_W // 2)
    sin_t = sin_t.reshape(b, t, QK_W // 2)
    tables = _retention_tables()

    row = lambda g: g.reshape(depth, 1, -1)
    bf = lambda w: w.astype(BF16)
    ffn1 = (row(ffn1_pre_g), bf(ffn1_w_gate), bf(ffn1_w_up), bf(ffn1_w_down), row(ffn1_post_g))
    ffn2 = (row(ffn2_pre_g), bf(ffn2_w_gate), bf(ffn2_w_up), bf(ffn2_w_down), row(ffn2_post_g))
    w_a2_p = bf(jnp.pad(_dup_cols(w_a2), ((0, 0), (0, GATE_RANK_PAD - GATE_RANK), (0, 0))))
    mix = (row(mix_pre_g), _relayout_w_in(bf(w_in)), w_a2_p, row(_dup_cols(b_a2)),
           row(ret_norm_g), row(gla_norm_g), bf(w_out), row(mix_post_g))
    xa_kv = (row(xa_mem_g), bf(xa_w_kv))
    xa = (row(xa_pre_g), bf(xa_w_q), bf(xa_w_o), row(xa_post_g))

    for l in range(depth):
        x = _ffn(x.reshape(b * t, d), l, *ffn1).reshape(b, t, d)
        x = _mixer(x, l, cos_t, sin_t, *mix, tables)
        k_mem, v_mem = _memkv(mem, l, *xa_kv)
        x = _xattn(x, l, k_mem, v_mem, *xa)
        x = _ffn(x.reshape(b * t, d), l, *ffn2).reshape(b, t, d)
    return x
```

```python
import numpy as np
import jax
import jax.numpy as jnp
from jax import lax
from jax.experimental import pallas as pl
from jax.experimental.pallas import tpu as pltpu

D_MODEL = 1024
D_FF = 2816
EPS = 1e-6
ROPE_BASE = 10000.0

LANES = 128
HEADS = 4
DK = 64
DV = 128
QK_W = HEADS * DK
V_W = HEADS * DV
CHUNK = 128
GLA_SUB = 32
N_SUB = CHUNK // GLA_SUB
GATE_RANK = 16
GATE_RANK_PAD = 128
GATE_TAU = 16.0
QK_SCALE = DK ** -0.5

XA_HEADS = 4
XA_DH = D_MODEL // XA_HEADS

OFF_GA = 0
OFF_GK = OFF_GA + GATE_RANK_PAD
OFF_GQ = OFF_GK + 2 * QK_W
OFF_GV = OFF_GQ + 2 * QK_W
OFF_RK = OFF_GV + V_W
OFF_RQ = OFF_RK + QK_W
OFF_RV = OFF_RQ + QK_W
OFF_GG = OFF_RV + V_W
OFF_RG = OFF_GG + V_W
IN_W = OFF_RG + V_W
N_GROUPS = 4
IN_GROUPS = (0, OFF_GV, OFF_RV, OFF_RG, IN_W)
OUT_GROUP = D_MODEL // N_GROUPS
LOG2E = 1.4426950408889634

FFN_TILE = 1024
FFN_SUB = 512
FFN_CHUNK = 256
MIX_TILE = 1024
MIX_SUB = 512
XA_TILE = 1024
XA_SUB = 512
ROPE_TILE = 1024
VMEM_LIMIT = 56 * 1024 * 1024

F32 = jnp.float32
BF16 = jnp.bfloat16


def _rms(x, g):
    return x * lax.rsqrt(jnp.mean(x * x, axis=-1, keepdims=True) + EPS) * g


def _silu(x):
    return x * (1.0 / (1.0 + jnp.exp(-x)))


def _dot(a, b):
    return jnp.dot(a, b, preferred_element_type=F32)


def _dot_nt(a, b):
    return lax.dot_general(a, b, (((1,), (1,)), ((), ())), preferred_element_type=F32)


def _block_diag2(a, b):
    za = jnp.zeros(a.shape, a.dtype)
    return jnp.concatenate(
        [jnp.concatenate([a, za], axis=-1), jnp.concatenate([za, b], axis=-1)], axis=0)


def _ffn_kernel(x_ref, pre_g_ref, wg_ref, wu_ref, wd_ref, post_g_ref, o_ref, u_ref):
    for s in range(FFN_TILE // FFN_SUB):
        rows = slice(s * FFN_SUB, (s + 1) * FFN_SUB)
        x = x_ref[rows, :]
        h = _rms(x, pre_g_ref[...]).astype(BF16)
        for lo in range(0, D_FF, FFN_CHUNK):
            a = _dot(h, wg_ref[:, lo:lo + FFN_CHUNK])
            b = _dot(h, wu_ref[:, lo:lo + FFN_CHUNK])
            u_ref[rows, lo:lo + FFN_CHUNK] = (_silu(a) * b).astype(BF16)
        acc = _dot(u_ref[rows, :], wd_ref[...])
        o_ref[rows, :] = x + _rms(acc, 0.5 * post_g_ref[...])


def _ffn(x2d, layer, pre_g, wg, wu, wd, post_g):
    n = x2d.shape[0]
    lay2 = lambda i: (layer, 0, 0)
    return pl.pallas_call(
        _ffn_kernel,
        grid=(n // FFN_TILE,),
        in_specs=[
            pl.BlockSpec((FFN_TILE, D_MODEL), lambda i: (i, 0)),
            pl.BlockSpec((None, 1, D_MODEL), lay2),
            pl.BlockSpec((None, D_MODEL, D_FF), lay2, pipeline_mode=pl.Buffered(1)),
            pl.BlockSpec((None, D_MODEL, D_FF), lay2, pipeline_mode=pl.Buffered(1)),
            pl.BlockSpec((None, D_FF, D_MODEL), lay2, pipeline_mode=pl.Buffered(1)),
            pl.BlockSpec((None, 1, D_MODEL), lay2),
        ],
        out_specs=pl.BlockSpec((FFN_TILE, D_MODEL), lambda i: (i, 0)),
        out_shape=jax.ShapeDtypeStruct(x2d.shape, F32),
        scratch_shapes=[pltpu.VMEM((FFN_TILE, D_FF), BF16)],
        compiler_params=pltpu.CompilerParams(
            dimension_semantics=("arbitrary",), vmem_limit_bytes=VMEM_LIMIT),
        name="ffn",
    )(x2d, pre_g, wg, wu, wd, post_g)


def _rope_kernel(pos_ref, freq_ref, cos_ref, sin_ref):
    ang = freq_ref[...] * pos_ref[...]
    cos_ref[...] = jnp.concatenate([jnp.cos(ang)] * HEADS, axis=0).T
    sin_ref[...] = jnp.concatenate([jnp.sin(ang)] * HEADS, axis=0).T


def _rope_tables(positions):
    b, t = positions.shape
    n = b * t
    pos = positions.astype(F32).reshape(n // ROPE_TILE, 1, ROPE_TILE)
    inv_freq = ROPE_BASE ** (-jnp.arange(0, DK, 2, dtype=F32) / DK)
    freq = inv_freq.reshape(DK // 2, 1)
    out = jax.ShapeDtypeStruct((n, QK_W // 2), F32)
    return pl.pallas_call(
        _rope_kernel,
        grid=(n // ROPE_TILE,),
        in_specs=[
            pl.BlockSpec((None, 1, ROPE_TILE), lambda i: (i, 0, 0)),
            pl.BlockSpec((DK // 2, 1), lambda i: (0, 0)),
        ],
        out_specs=[pl.BlockSpec((ROPE_TILE, QK_W // 2), lambda i: (i, 0))] * 2,
        out_shape=[out, out],
        compiler_params=pltpu.CompilerParams(dimension_semantics=("arbitrary",)),
        name="rope_tables",
    )(pos, freq)


def _mixer_kernel(x_ref, cos_ref, sin_ref, pre_g_ref, w_in_ref, w_a2_ref, b_a2_ref,
                  ret_g_ref, gla_g_ref, w_out_ref, post_g_ref,
                  dmat_ref, qdec_ref, kdec_ref, cdec_ref,
                  o_ref,
                  h_ref, z_ref, la_ref, y_ref, rstate_ref, rstate_bd_ref, gstate_ref):
    tile = x_ref.shape[0]

    @pl.when(pl.program_id(1) == 0)
    def _():
        rstate_ref[...] = jnp.zeros(rstate_ref.shape, F32)
        rstate_bd_ref[...] = jnp.zeros(rstate_bd_ref.shape, BF16)
        gstate_ref[...] = jnp.zeros(gstate_ref.shape, F32)

    def pre_norm(srows):
        h_ref[srows, :] = _rms(x_ref[srows, :], pre_g_ref[...]).astype(BF16)

    def in_projection(srows, g):
        lo, hi = IN_GROUPS[g], IN_GROUPS[g + 1]
        z_ref[srows, lo:hi] = _dot(h_ref[srows, :], w_in_ref[:, lo:hi])
        if g == 0:
            ga = z_ref[srows, OFF_GA:OFF_GA + GATE_RANK_PAD].astype(BF16)
            pre = _dot(ga, w_a2_ref[...]) + b_a2_ref[...]
            la_ref[srows, :] = ((jnp.minimum(pre, 0.0) - jnp.log1p(jnp.exp(-jnp.abs(pre))))
                                * (LOG2E / GATE_TAU))

    def out_projection(srows, g):
        cols = slice(g * OUT_GROUP, (g + 1) * OUT_GROUP)
        o_ref[srows, cols] = _dot(y_ref[srows, :], w_out_ref[:, cols])

    def post_norm(srows):
        o_ref[srows, :] = x_ref[srows, :] + _rms(o_ref[srows, :], post_g_ref[...])

    half = QK_W // 2
    hw = DK // 2
    lane_qk = lax.broadcasted_iota(jnp.int32, (1, QK_W), 1)
    ret_masks = [(lane_qk % half) // hw == hd for hd in range(HEADS)]

    ri = lax.broadcasted_iota(jnp.int32, (CHUNK, CHUNK), 0)
    ci = lax.broadcasted_iota(jnp.int32, (CHUNK, CHUNK), 1)
    causal = ri >= ci
    tri_bf = jnp.where(causal, 1.0, 0.0).astype(BF16)
    lane_v = lax.broadcasted_iota(jnp.int32, (1, V_W), 1)
    dup_half = (lane_v % LANES) // DK
    first_half = lax.broadcasted_iota(jnp.int32, (1, LANES), 1) < DK

    def natural_pair(a, p):
        return jnp.where(first_half, a[:, 2 * p * LANES:(2 * p + 1) * LANES],
                         a[:, (2 * p + 1) * LANES:(2 * p + 2) * LANES])

    blk = lambda a, rb: a[rb * GLA_SUB:(rb + 1) * GLA_SUB]

    def stage_a(c):
        rows = slice(c * CHUNK, (c + 1) * CHUNK)
        out = {}

        cosv = cos_ref[rows, :]
        sinv = sin_ref[rows, :]

        def rope(off):
            x1 = z_ref[rows, off:off + half]
            x2 = z_ref[rows, off + half:off + QK_W]
            return jnp.concatenate([x1 * cosv - x2 * sinv, x1 * sinv + x2 * cosv], axis=-1)

        q = rope(OFF_RQ)
        k = rope(OFF_RK)
        v = z_ref[rows, OFF_RV:OFF_RV + V_W].astype(BF16)
        qstack = jnp.concatenate([jnp.where(m, q, 0.0) for m in ret_masks], axis=0).astype(BF16)
        s_all = (_dot_nt(qstack, k.astype(BF16)) * dmat_ref[...]).astype(BF16)
        kdt = (k * kdec_ref[...]).T
        out["r_v"] = v
        out["r_s"] = s_all
        out["r_qd"] = (q * qdec_ref[...]).astype(BF16)
        out["r_kd"] = [jnp.concatenate([kdt[hd * hw:(hd + 1) * hw],
                                        kdt[half + hd * hw:half + (hd + 1) * hw]], axis=0).astype(BF16)
                       for hd in range(HEADS)]

        la = la_ref[rows, :]
        a1 = la.astype(BF16)
        a2 = (la - a1.astype(F32)).astype(BF16)
        cum = _dot(tri_bf, a1) + _dot(tri_bf, a2)
        gq = z_ref[rows, OFF_GQ:OFF_GQ + 2 * QK_W]
        gk = z_ref[rows, OFF_GK:OFF_GK + 2 * QK_W]
        gv = z_ref[rows, OFF_GV:OFF_GV + V_W].astype(BF16)

        mid = GLA_SUB // 2 - 1
        refs = [cum[s * GLA_SUB + mid:s * GLA_SUB + mid + 1, :] for s in range(N_SUB)]
        last = cum[CHUNK - 1:CHUNK, :]
        d = cum - jnp.concatenate([jnp.broadcast_to(r, (GLA_SUB, V_W)) for r in refs], axis=0)
        gq_e = gq * jnp.exp2(d)
        gk_e = gk * jnp.exp2(-d)
        zero_blk = jnp.zeros((GLA_SUB, V_W), BF16)
        q_sets, k_sets = [], []
        for t2 in range(N_SUB // 2):
            q_rows, k_rows = [], []
            for rb in range(N_SUB):
                s_lo, s_hi = 2 * t2, 2 * t2 + 1
                if s_lo > rb:
                    q_rows.append(zero_blk)
                else:
                    f_lo = jnp.exp2(refs[rb] - refs[s_lo])
                    f_hi = jnp.exp2(refs[rb] - refs[s_hi]) if s_hi <= rb else jnp.zeros_like(f_lo)
                    q_rows.append((blk(gq_e, rb) * jnp.where(dup_half == 0, f_lo, f_hi)).astype(BF16))
                if rb // 2 == t2:
                    k_rows.append(jnp.where(dup_half == rb % 2, blk(gk_e, rb), 0.0).astype(BF16))
                else:
                    k_rows.append(zero_blk)
            q_sets.append(jnp.concatenate(q_rows, axis=0))
            k_sets.append(jnp.concatenate(k_rows, axis=0))

        e_last = jnp.exp2(last)
        out["g_v"] = gv
        out["g_q"] = [jnp.concatenate([qs[:, hd * LANES:(hd + 1) * LANES] for qs in q_sets], axis=-1)
                      for hd in range(HEADS)]
        out["g_k"] = [jnp.concatenate([ks[:, hd * LANES:(hd + 1) * LANES] for ks in k_sets], axis=-1)
                      for hd in range(HEADS)]
        out["g_qe"], out["g_kdt"], out["g_dec"] = [], [], []
        for p in range(HEADS // 2):
            gq_n = natural_pair(gq_e, p)
            gk_n = natural_pair(gk_e, p)
            out["g_qe"].append(jnp.concatenate(
                [blk(gq_n, rb) * natural_pair(jnp.exp2(refs[rb]), p) for rb in range(N_SUB)],
                axis=0).astype(BF16))
            kd = jnp.concatenate(
                [blk(gk_n, rb) * natural_pair(jnp.exp2(last - refs[rb]), p) for rb in range(N_SUB)],
                axis=0)
            out["g_kdt"].append(kd.T.astype(BF16))
            out["g_dec"].append(jnp.broadcast_to(natural_pair(e_last, p), (CHUNK, LANES)).T)
        return out

    def stage_b(c, a):
        rows = slice(c * CHUNK, (c + 1) * CHUNK)
        g_s = [jnp.where(causal, _dot_nt(a["g_q"][hd], a["g_k"][hd]), 0.0).astype(BF16)
               for hd in range(HEADS)]

        v, s_all = a["r_v"], a["r_s"]
        o_inter = _dot(a["r_qd"], rstate_bd_ref[...])
        for p in range(HEADS // 2):
            ha, hb = 2 * p, 2 * p + 1
            s_pair = jnp.concatenate([s_all[ha * CHUNK:(ha + 1) * CHUNK],
                                      s_all[hb * CHUNK:(hb + 1) * CHUNK]], axis=-1)
            o_pair = _dot(s_pair, _block_diag2(v[:, ha * DV:(ha + 1) * DV], v[:, hb * DV:(hb + 1) * DV]))
            for hd in (ha, hb):
                cols = slice(hd * DV, (hd + 1) * DV)
                o = o_pair[:, (hd - ha) * DV:(hd - ha + 1) * DV] + o_inter[:, cols]
                o = _rms(o, ret_g_ref[:, cols])
                gate = _silu(z_ref[rows, OFF_RG + hd * DV:OFF_RG + (hd + 1) * DV])
                y_ref[rows, cols] = (o * gate).astype(BF16)
                st = rstate_ref[hd] * cdec_ref[hd] + _dot(a["r_kd"][hd], v[:, cols])
                rstate_ref[hd] = st
                st_bf = st.astype(BF16)
                rstate_bd_ref[hd * hw:(hd + 1) * hw, cols] = st_bf[:hw]
                rstate_bd_ref[half + hd * hw:half + (hd + 1) * hw, cols] = st_bf[hw:]

        gv = a["g_v"]
        for p in range(HEADS // 2):
            ha, hb = 2 * p, 2 * p + 1
            st = gstate_ref[p]
            o_pair = (_dot(a["g_qe"][p], st.astype(BF16))
                      + _dot(jnp.concatenate([g_s[ha], g_s[hb]], axis=-1),
                             _block_diag2(gv[:, ha * DV:(ha + 1) * DV], gv[:, hb * DV:(hb + 1) * DV])))
            dec_col = a["g_dec"][p]
            st = st * jnp.concatenate([dec_col, dec_col], axis=-1)
            gstate_ref[p] = st
            kdt = a["g_kdt"][p]
            for hd in (ha, hb):
                j = hd - ha
                cols = slice(hd * DV, (hd + 1) * DV)
                upd = _dot(kdt[j * DK:(j + 1) * DK], gv[:, cols])
                gstate_ref[p, j * DK:(j + 1) * DK, j * DV:(j + 1) * DV] = (
                    st[j * DK:(j + 1) * DK, j * DV:(j + 1) * DV] + upd)
                o = _rms(o_pair[:, j * DV:(j + 1) * DV], gla_g_ref[:, cols])
                gate = _silu(z_ref[rows, OFF_GG + hd * DV:OFF_GG + (hd + 1) * DV])
                y_ref[rows, V_W + hd * DV:V_W + (hd + 1) * DV] = (o * gate).astype(BF16)

    n_sub_chunks = MIX_SUB // CHUNK
    assert tile == 2 * MIX_SUB and n_sub_chunks == N_GROUPS
    rows_a, rows_b = slice(0, MIX_SUB), slice(MIX_SUB, tile)
    pre_norm(rows_a)
    for g in range(N_GROUPS):
        in_projection(rows_a, g)
    pre_norm(rows_b)
    n_chunks = 2 * n_sub_chunks
    fillers = {0: [(in_projection, rows_b, 0)],
               1: [(in_projection, rows_b, 1), (in_projection, rows_b, 2)],
               2: [(in_projection, rows_b, 3)],
               4: [(out_projection, rows_a, 0)],
               5: [(out_projection, rows_a, 1)],
               6: [(out_projection, rows_a, 2)],
               7: [(out_projection, rows_a, 3)]}
    ahead = stage_a(0)
    for c in range(n_chunks):
        cur = ahead
        if c + 1 < n_chunks:
            ahead = stage_a(c + 1)
        stage_b(c, cur)
        for fn, srows, g in fillers.get(c, []):
            fn(srows, g)
    post_norm(rows_a)
    for g in range(N_GROUPS):
        out_projection(rows_b, g)
    post_norm(rows_b)


def _mixer(x, layer, cos_t, sin_t, pre_g, w_in, w_a2, b_a2, ret_g, gla_g, w_out, post_g, tables):
    b, t, d = x.shape
    dmat, qdec, kdec, cdec = tables
    const2 = lambda bi, ti: (0, 0)
    lay2 = lambda bi, ti: (layer, 0, 0)
    tok = lambda bi, ti: (bi, ti, 0)
    return pl.pallas_call(
        _mixer_kernel,
        grid=(b, t // MIX_TILE),
        in_specs=[
            pl.BlockSpec((None, MIX_TILE, d), tok),
            pl.BlockSpec((None, MIX_TILE, QK_W // 2), tok),
            pl.BlockSpec((None, MIX_TILE, QK_W // 2), tok),
            pl.BlockSpec((None, 1, d), lay2),
            pl.BlockSpec((None, d, IN_W), lay2, pipeline_mode=pl.Buffered(1)),
            pl.BlockSpec((None, GATE_RANK_PAD, 2 * QK_W), lay2),
            pl.BlockSpec((None, 1, 2 * QK_W), lay2),
            pl.BlockSpec((None, 1, V_W), lay2),
            pl.BlockSpec((None, 1, V_W), lay2),
            pl.BlockSpec((None, 2 * V_W, d), lay2, pipeline_mode=pl.Buffered(1)),
            pl.BlockSpec((None, 1, d), lay2),
            pl.BlockSpec((HEADS * CHUNK, CHUNK), const2),
            pl.BlockSpec((CHUNK, QK_W), const2),
            pl.BlockSpec((CHUNK, QK_W), const2),
            pl.BlockSpec(memory_space=pltpu.SMEM),
        ],
        out_specs=pl.BlockSpec((None, MIX_TILE, d), tok),
        out_shape=jax.ShapeDtypeStruct(x.shape, F32),
        scratch_shapes=[
            pltpu.VMEM((MIX_TILE, d), BF16),
            pltpu.VMEM((MIX_TILE, IN_W), F32),
            pltpu.VMEM((MIX_TILE, 2 * QK_W), F32),
            pltpu.VMEM((MIX_TILE, 2 * V_W), BF16),
            pltpu.VMEM((HEADS, DK, DV), F32),
            pltpu.VMEM((QK_W, V_W), BF16),
            pltpu.VMEM((HEADS // 2, 2 * DK, 2 * DV), F32),
        ],
        compiler_params=pltpu.CompilerParams(
            dimension_semantics=("arbitrary", "arbitrary"), vmem_limit_bytes=VMEM_LIMIT),
        name="mixer",
    )(x, cos_t, sin_t, pre_g, w_in, w_a2, b_a2, ret_g, gla_g, w_out, post_g,
      dmat, qdec, kdec, cdec)


def _retention_tables():
    c = CHUNK
    log_gamma = jnp.log1p(-jnp.exp2(-5.0 - jnp.arange(HEADS, dtype=F32)))
    idx = jnp.arange(c, dtype=F32)
    rel = idx[:, None] - idx[None, :]
    causal = rel >= 0
    decay_in = jnp.where(causal[None],
                         jnp.exp(jnp.where(causal, rel, 0.0)[None] * log_gamma[:, None, None]), 0.0)
    q_dec = jnp.exp((idx + 1.0)[None, :] * log_gamma[:, None])
    k_dec = jnp.exp((c - 1.0 - idx)[None, :] * log_gamma[:, None])
    chunk_dec = jnp.exp(c * log_gamma)
    head_of_lane = (np.arange(QK_W) % (QK_W // 2)) // (DK // 2)
    dmat = decay_in.reshape(HEADS * c, c)
    qdec = q_dec.T[:, head_of_lane]
    kdec = k_dec.T[:, head_of_lane]
    return dmat, qdec, kdec, chunk_dec


def _rotary_cols(w):
    lead = w.shape[:-1]
    w = w.reshape(lead + (HEADS, 2, DK // 2))
    return jnp.swapaxes(w, -3, -2).reshape(lead + (QK_W,))


def _dup_cols(w):
    lead = w.shape[:-1]
    w = w.reshape(lead + (HEADS, 1, DK))
    return jnp.broadcast_to(w, lead + (HEADS, 2, DK)).reshape(lead + (2 * QK_W,))


def _relayout_w_in(w_in):
    sizes = (QK_W, QK_W, V_W, V_W, QK_W, QK_W, V_W, GATE_RANK, V_W)
    pts = np.cumsum(sizes)[:-1]
    rq, rk, rv, rg, gq, gk, gv, ga, gg = jnp.split(w_in, pts, axis=-1)
    ga = jnp.pad(ga, ((0, 0), (0, 0), (0, GATE_RANK_PAD - GATE_RANK)))
    return jnp.concatenate([ga, _dup_cols(gk), _dup_cols(gq) * QK_SCALE, gv,
                            _rotary_cols(rk) * QK_SCALE, _rotary_cols(rq), rv, gg, rg], axis=-1)


def _memkv_kernel(mem_ref, g_ref, w_ref, k_ref, v_ref):
    m = _rms(mem_ref[...], g_ref[...]).astype(BF16)
    kv = _dot(m, w_ref[...])
    k_ref[...] = kv[:, :D_MODEL].T.astype(BF16)
    v_ref[...] = kv[:, D_MODEL:].astype(BF16)


def _memkv(mem, layer, g, w_kv):
    b, m, d = mem.shape
    lay2 = lambda i: (layer, 0, 0)
    return pl.pallas_call(
        _memkv_kernel,
        grid=(b,),
        in_specs=[
            pl.BlockSpec((None, m, d), lambda i: (i, 0, 0)),
            pl.BlockSpec((None, 1, d), lay2),
            pl.BlockSpec((None, d, 2 * d), lay2),
        ],
        out_specs=[pl.BlockSpec((None, d, m), lambda i: (i, 0, 0)),
                   pl.BlockSpec((None, m, d), lambda i: (i, 0, 0))],
        out_shape=[jax.ShapeDtypeStruct((b, d, m), BF16), jax.ShapeDtypeStruct((b, m, d), BF16)],
        compiler_params=pltpu.CompilerParams(
            dimension_semantics=("arbitrary",), vmem_limit_bytes=VMEM_LIMIT),
        name="memkv",
    )(mem, g, w_kv)


def _xattn_kernel(x_ref, kt_ref, v_ref, pre_g_ref, wq_ref, wo_ref, post_g_ref, o_ref, att_ref):
    for sub in range(XA_TILE // XA_SUB):
        rows = slice(sub * XA_SUB, (sub + 1) * XA_SUB)
        x = x_ref[rows, :]
        h = _rms(x, pre_g_ref[...]).astype(BF16)
        q = _dot(h, wq_ref[...])
        for hd in range(XA_HEADS):
            cols = slice(hd * XA_DH, (hd + 1) * XA_DH)
            s = _dot(q[:, cols].astype(BF16), kt_ref[cols, :]) * (XA_DH ** -0.5)
            p = jnp.exp(s - jnp.max(s, axis=-1, keepdims=True))
            o = _dot(p.astype(BF16), v_ref[:, cols]) / jnp.sum(p, axis=-1, keepdims=True)
            att_ref[rows, cols] = o.astype(BF16)
        out = _dot(att_ref[rows, :], wo_ref[...])
        o_ref[rows, :] = x + _rms(out, post_g_ref[...])


def _xattn(x, layer, kt, v, pre_g, wq, wo, post_g):
    b, t, d = x.shape
    m = v.shape[1]
    lay2 = lambda bi, ti: (layer, 0, 0)
    tok = lambda bi, ti: (bi, ti, 0)
    mem = lambda bi, ti: (bi, 0, 0)
    return pl.pallas_call(
        _xattn_kernel,
        grid=(b, t // XA_TILE),
        in_specs=[
            pl.BlockSpec((None, XA_TILE, d), tok),
            pl.BlockSpec((None, d, m), mem),
            pl.BlockSpec((None, m, d), mem),
            pl.BlockSpec((None, 1, d), lay2),
            pl.BlockSpec((None, d, d), lay2),
            pl.BlockSpec((None, d, d), lay2),
            pl.BlockSpec((None, 1, d), lay2),
        ],
        out_specs=pl.BlockSpec((None, XA_TILE, d), tok),
        out_shape=jax.ShapeDtypeStruct(x.shape, F32),
        scratch_shapes=[pltpu.VMEM((XA_TILE, d), BF16)],
        compiler_params=pltpu.CompilerParams(
            dimension_semantics=("arbitrary", "arbitrary"), vmem_limit_bytes=VMEM_LIMIT),
        name="xattn",
    )(x, kt, v, pre_g, wq, wo, post_g)


def kernel(x, mem, positions, ffn1_pre_g, ffn1_w_gate, ffn1_w_up, ffn1_w_down, ffn1_post_g, mix_pre_g, w_in, w_a2, b_a2, ret_norm_g, gla_norm_g, w_out, mix_post_g, xa_pre_g, xa_mem_g, xa_w_q, xa_w_kv, xa_w_o, xa_post_g, ffn2_pre_g, ffn2_w_gate, ffn2_w_up, ffn2_w_down, ffn2_post_g):
    b, t, d = x.shape
    depth = w_in.shape[0]
    assert d == D_MODEL and t % MIX_TILE == 0 and t % XA_TILE == 0 and (b * t) % FFN_TILE == 0
    assert (b * t) % ROPE_TILE == 0

    cos_t, sin_t = _rope_tables(positions)
    cos_t = cos_t.reshape(b, t, QK_W // 2)
    sin_t = sin_t.reshape(b, t, QK_W // 2)
    tables = _retention_tables()

    row = lambda g: g.reshape(depth, 1, -1)
    bf = lambda w: w.astype(BF16)
    ffn1 = (row(ffn1_pre_g), bf(ffn1_w_gate), bf(ffn1_w_up), bf(ffn1_w_down), row(ffn1_post_g))
    ffn2 = (row(ffn2_pre_g), bf(ffn2_w_gate), bf(ffn2_w_up), bf(ffn2_w_down), row(ffn2_post_g))
    w_a2_p = bf(jnp.pad(_dup_cols(w_a2), ((0, 0), (0, GATE_RANK_PAD - GATE_RANK), (0, 0))))
    mix = (row(mix_pre_g), _relayout_w_in(bf(w_in)), w_a2_p, row(_dup_cols(b_a2)),
           row(ret_norm_g), row(gla_norm_g), bf(w_out), row(mix_post_g))
    xa_kv = (row(xa_mem_g), bf(xa_w_kv))
    xa = (row(xa_pre_g), bf(xa_w_q), bf(xa_w_o), row(xa_post_g))

    for l in range(depth):
        x = _ffn(x.reshape(b * t, d), l, *ffn1).reshape(b, t, d)
        x = _mixer(x, l, cos_t, sin_t, *mix, tables)
        k_mem, v_mem = _memkv(mem, l, *xa_kv)
        x = _xattn(x, l, k_mem, v_mem, *xa)
        x = _ffn(x.reshape(b * t, d), l, *ffn2).reshape(b, t, d)
    return x
```

```python
import numpy as np
import jax
import jax.numpy as jnp
from jax import lax
from jax.experimental import pallas as pl
from jax.experimental.pallas import tpu as pltpu

D_MODEL = 1024
D_FF = 2816
EPS = 1e-6
ROPE_BASE = 10000.0

LANES = 128
HEADS = 4
DK = 64
DV = 128
QK_W = HEADS * DK
V_W = HEADS * DV
CHUNK = 128
GLA_SUB = 32
N_SUB = CHUNK // GLA_SUB
GATE_RANK = 16
GATE_RANK_PAD = 128
GATE_TAU = 16.0
QK_SCALE = DK ** -0.5

XA_HEADS = 4
XA_DH = D_MODEL // XA_HEADS
XA_SCALE = XA_DH ** -0.5
assert XA_SCALE == 2.0 ** -4 and QK_SCALE == 2.0 ** -3

OFF_GA = 0
OFF_GK = OFF_GA + GATE_RANK_PAD
OFF_GQ = OFF_GK + 2 * QK_W
OFF_GV = OFF_GQ + 2 * QK_W
OFF_RK = OFF_GV + V_W
OFF_RQ = OFF_RK + QK_W
OFF_RV = OFF_RQ + QK_W
OFF_GG = OFF_RV + V_W
OFF_RG = OFF_GG + V_W
IN_W = OFF_RG + V_W
N_GROUPS = 4
IN_GROUPS = (0, OFF_GV, OFF_RV, OFF_RG, IN_W)
OUT_GROUP = D_MODEL // N_GROUPS
LOG2E = 1.4426950408889634

FFN_TILE = 1024
FFN_SUB = 512
FFN_CHUNK = 256
MIX_TILE = 1024
MIX_SUB = 512
XA_TILE = 1024
XA_SUB = 512
ROPE_TILE = 1024
MEMKV_BATCH = 4
VMEM_LIMIT = 56 * 1024 * 1024

F32 = jnp.float32
BF16 = jnp.bfloat16


def _rms(x, g):
    return x * lax.rsqrt(jnp.mean(x * x, axis=-1, keepdims=True) + EPS) * g


def _silu(x):
    return x * (1.0 / (1.0 + jnp.exp(-x)))


def _dot(a, b):
    return jnp.dot(a, b, preferred_element_type=F32)


def _dot_nt(a, b):
    return lax.dot_general(a, b, (((1,), (1,)), ((), ())), preferred_element_type=F32)


def _block_diag2(a, b):
    za = jnp.zeros(a.shape, a.dtype)
    return jnp.concatenate(
        [jnp.concatenate([a, za], axis=-1), jnp.concatenate([za, b], axis=-1)], axis=0)


def _ffn_kernel(x_ref, pre_g_ref, wg_ref, wu_ref, wd_ref, post_g_ref, o_ref, u_ref):
    for s in range(FFN_TILE // FFN_SUB):
        rows = slice(s * FFN_SUB, (s + 1) * FFN_SUB)
        x = x_ref[rows, :]
        h = _rms(x, pre_g_ref[...]).astype(BF16)
        for lo in range(0, D_FF, FFN_CHUNK):
            a = _dot(h, wg_ref[:, lo:lo + FFN_CHUNK])
            b = _dot(h, wu_ref[:, lo:lo + FFN_CHUNK])
            u_ref[rows, lo:lo + FFN_CHUNK] = (_silu(a) * b).astype(BF16)
        acc = _dot(u_ref[rows, :], wd_ref[...])
        o_ref[rows, :] = x + _rms(acc, 0.5 * post_g_ref[...])


def _ffn(x2d, layer, pre_g, wg, wu, wd, post_g):
    n = x2d.shape[0]
    lay2 = lambda i: (layer, 0, 0)
    return pl.pallas_call(
        _ffn_kernel,
        grid=(n // FFN_TILE,),
        in_specs=[
            pl.BlockSpec((FFN_TILE, D_MODEL), lambda i: (i, 0)),
            pl.BlockSpec((None, 1, D_MODEL), lay2),
            pl.BlockSpec((None, D_MODEL, D_FF), lay2, pipeline_mode=pl.Buffered(1)),
            pl.BlockSpec((None, D_MODEL, D_FF), lay2, pipeline_mode=pl.Buffered(1)),
            pl.BlockSpec((None, D_FF, D_MODEL), lay2, pipeline_mode=pl.Buffered(1)),
            pl.BlockSpec((None, 1, D_MODEL), lay2),
        ],
        out_specs=pl.BlockSpec((FFN_TILE, D_MODEL), lambda i: (i, 0)),
        out_shape=jax.ShapeDtypeStruct(x2d.shape, F32),
        scratch_shapes=[pltpu.VMEM((FFN_TILE, D_FF), BF16)],
        compiler_params=pltpu.CompilerParams(
            dimension_semantics=("arbitrary",), vmem_limit_bytes=VMEM_LIMIT),
        name="ffn",
    )(x2d, pre_g, wg, wu, wd, post_g)


def _rope_kernel(pos_ref, freq_ref, cos_ref, sin_ref):
    ang = freq_ref[...] * pos_ref[...]
    cos_ref[...] = jnp.concatenate([jnp.cos(ang)] * HEADS, axis=0).T
    sin_ref[...] = jnp.concatenate([jnp.sin(ang)] * HEADS, axis=0).T


def _rope_tables(positions):
    b, t = positions.shape
    n = b * t
    pos = positions.astype(F32).reshape(n // ROPE_TILE, 1, ROPE_TILE)
    inv_freq = ROPE_BASE ** (-jnp.arange(0, DK, 2, dtype=F32) / DK)
    freq = inv_freq.reshape(DK // 2, 1)
    out = jax.ShapeDtypeStruct((n, QK_W // 2), F32)
    return pl.pallas_call(
        _rope_kernel,
        grid=(n // ROPE_TILE,),
        in_specs=[
            pl.BlockSpec((None, 1, ROPE_TILE), lambda i: (i, 0, 0)),
            pl.BlockSpec((DK // 2, 1), lambda i: (0, 0)),
        ],
        out_specs=[pl.BlockSpec((ROPE_TILE, QK_W // 2), lambda i: (i, 0))] * 2,
        out_shape=[out, out],
        compiler_params=pltpu.CompilerParams(dimension_semantics=("arbitrary",)),
        name="rope_tables",
    )(pos, freq)


def _mixer_kernel(x_ref, cos_ref, sin_ref, pre_g_ref, w_in_ref, w_a2_ref, b_a2_ref,
                  ret_g_ref, gla_g_ref, w_out_ref, post_g_ref,
                  dmat_ref, qdec_ref, kdec_ref, cdec_ref,
                  o_ref,
                  h_ref, z_ref, la_ref, y_ref, rstate_ref, rstate_bd_ref, gstate_ref):
    tile = x_ref.shape[0]

    @pl.when(pl.program_id(1) == 0)
    def _():
        rstate_ref[...] = jnp.zeros(rstate_ref.shape, F32)
        rstate_bd_ref[...] = jnp.zeros(rstate_bd_ref.shape, BF16)
        gstate_ref[...] = jnp.zeros(gstate_ref.shape, F32)

    def pre_norm(srows):
        h_ref[srows, :] = _rms(x_ref[srows, :], pre_g_ref[...]).astype(BF16)

    def in_projection(srows, g):
        lo, hi = IN_GROUPS[g], IN_GROUPS[g + 1]
        z_ref[srows, lo:hi] = _dot(h_ref[srows, :], w_in_ref[:, lo:hi])
        if g == 0:
            ga = z_ref[srows, OFF_GA:OFF_GA + GATE_RANK_PAD].astype(BF16)
            pre = _dot(ga, w_a2_ref[...]) + b_a2_ref[...]
            la_ref[srows, :] = ((jnp.minimum(pre, 0.0) - jnp.log1p(jnp.exp(-jnp.abs(pre))))
                                * (LOG2E / GATE_TAU))
        if g == N_GROUPS - 1:
            z_ref[srows, OFF_GG:IN_W] = _silu(z_ref[srows, OFF_GG:IN_W])

    def out_projection(srows, g):
        cols = slice(g * OUT_GROUP, (g + 1) * OUT_GROUP)
        o_ref[srows, cols] = _dot(y_ref[srows, :], w_out_ref[:, cols])

    def post_norm(srows):
        o_ref[srows, :] = x_ref[srows, :] + _rms(o_ref[srows, :], post_g_ref[...])

    half = QK_W // 2
    hw = DK // 2
    lane_qk = lax.broadcasted_iota(jnp.int32, (1, QK_W), 1)
    ret_masks = [(lane_qk % half) // hw == hd for hd in range(HEADS)]

    ri = lax.broadcasted_iota(jnp.int32, (CHUNK, CHUNK), 0)
    ci = lax.broadcasted_iota(jnp.int32, (CHUNK, CHUNK), 1)
    causal = ri >= ci
    tri_bf = jnp.where(causal, 1.0, 0.0).astype(BF16)
    tri2_bf = jnp.concatenate([tri_bf, tri_bf], axis=1)
    lane_v = lax.broadcasted_iota(jnp.int32, (1, V_W), 1)
    dup_half = (lane_v % LANES) // DK
    first_half = lax.broadcasted_iota(jnp.int32, (1, LANES), 1) < DK

    def natural_pair(a, p):
        return jnp.where(first_half, a[:, 2 * p * LANES:(2 * p + 1) * LANES],
                         a[:, (2 * p + 1) * LANES:(2 * p + 2) * LANES])

    blk = lambda a, rb: a[rb * GLA_SUB:(rb + 1) * GLA_SUB]

    def stage_a(c):
        rows = slice(c * CHUNK, (c + 1) * CHUNK)
        out = {}

        cosv = cos_ref[rows, :]
        sinv = sin_ref[rows, :]

        def rope(off):
            x1 = z_ref[rows, off:off + half]
            x2 = z_ref[rows, off + half:off + QK_W]
            return jnp.concatenate([x1 * cosv - x2 * sinv, x1 * sinv + x2 * cosv], axis=-1)

        q = rope(OFF_RQ)
        k = rope(OFF_RK)
        v = z_ref[rows, OFF_RV:OFF_RV + V_W].astype(BF16)
        qstack = jnp.concatenate([jnp.where(m, q, 0.0) for m in ret_masks], axis=0).astype(BF16)
        s_all = (_dot_nt(qstack, k.astype(BF16)) * dmat_ref[...]).astype(BF16)
        kdt = (k * kdec_ref[...]).T
        out["r_v"] = v
        out["r_s"] = s_all
        out["r_qd"] = (q * qdec_ref[...]).astype(BF16)
        out["r_kd"] = [jnp.concatenate([kdt[hd * hw:(hd + 1) * hw],
                                        kdt[half + hd * hw:half + (hd + 1) * hw]], axis=0).astype(BF16)
                       for hd in range(HEADS)]

        la = la_ref[rows, :]
        a1 = la.astype(BF16)
        a2 = (la - a1.astype(F32)).astype(BF16)
        cum = _dot(tri2_bf, jnp.concatenate([a1, a2], axis=0))
        gq = z_ref[rows, OFF_GQ:OFF_GQ + 2 * QK_W]
        gk = z_ref[rows, OFF_GK:OFF_GK + 2 * QK_W]
        gv = z_ref[rows, OFF_GV:OFF_GV + V_W].astype(BF16)

        mid = GLA_SUB // 2 - 1
        refs = [cum[s * GLA_SUB + mid:s * GLA_SUB + mid + 1, :] for s in range(N_SUB)]
        last = cum[CHUNK - 1:CHUNK, :]
        d = cum - jnp.concatenate([jnp.broadcast_to(r, (GLA_SUB, V_W)) for r in refs], axis=0)
        gq_e = gq * jnp.exp2(d)
        gk_e = gk * jnp.exp2(-d)
        zero_blk = jnp.zeros((GLA_SUB, V_W), BF16)
        q_sets, k_sets = [], []
        for t2 in range(N_SUB // 2):
            q_rows, k_rows = [], []
            for rb in range(N_SUB):
                s_lo, s_hi = 2 * t2, 2 * t2 + 1
                if s_lo > rb:
                    q_rows.append(zero_blk)
                else:
                    f_lo = jnp.exp2(refs[rb] - refs[s_lo])
                    f_hi = jnp.exp2(refs[rb] - refs[s_hi]) if s_hi <= rb else jnp.zeros_like(f_lo)
                    q_rows.append((blk(gq_e, rb) * jnp.where(dup_half == 0, f_lo, f_hi)).astype(BF16))
                if rb // 2 == t2:
                    k_rows.append(jnp.where(dup_half == rb % 2, blk(gk_e, rb), 0.0).astype(BF16))
                else:
                    k_rows.append(zero_blk)
            q_sets.append(jnp.concatenate(q_rows, axis=0))
            k_sets.append(jnp.concatenate(k_rows, axis=0))

        e_last = jnp.exp2(last)
        out["g_v"] = gv
        out["g_q"] = [jnp.concatenate([qs[:, hd * LANES:(hd + 1) * LANES] for qs in q_sets], axis=-1)
                      for hd in range(HEADS)]
        out["g_k"] = [jnp.concatenate([ks[:, hd * LANES:(hd + 1) * LANES] for ks in k_sets], axis=-1)
                      for hd in range(HEADS)]
        out["g_qe"], out["g_kdt"], out["g_dec"] = [], [], []
        for p in range(HEADS // 2):
            gq_n = natural_pair(gq_e, p)
            gk_n = natural_pair(gk_e, p)
            out["g_qe"].append(jnp.concatenate(
                [blk(gq_n, rb) * natural_pair(jnp.exp2(refs[rb]), p) for rb in range(N_SUB)],
                axis=0).astype(BF16))
            kd = jnp.concatenate(
                [blk(gk_n, rb) * natural_pair(jnp.exp2(last - refs[rb]), p) for rb in range(N_SUB)],
                axis=0)
            out["g_kdt"].append(kd.T.astype(BF16))
            out["g_dec"].append(jnp.broadcast_to(natural_pair(e_last, p), (CHUNK, LANES)).T)
        return out

    def stage_b(c, a):
        rows = slice(c * CHUNK, (c + 1) * CHUNK)
        g_s = [jnp.where(causal, _dot_nt(a["g_q"][hd], a["g_k"][hd]), 0.0).astype(BF16)
               for hd in range(HEADS)]

        v, s_all = a["r_v"], a["r_s"]
        o_inter = _dot(a["r_qd"], rstate_bd_ref[...])
        for p in range(HEADS // 2):
            ha, hb = 2 * p, 2 * p + 1
            s_pair = jnp.concatenate([s_all[ha * CHUNK:(ha + 1) * CHUNK],
                                      s_all[hb * CHUNK:(hb + 1) * CHUNK]], axis=-1)
            o_pair = _dot(s_pair, _block_diag2(v[:, ha * DV:(ha + 1) * DV], v[:, hb * DV:(hb + 1) * DV]))
            for hd in (ha, hb):
                cols = slice(hd * DV, (hd + 1) * DV)
                o = o_pair[:, (hd - ha) * DV:(hd - ha + 1) * DV] + o_inter[:, cols]
                o = _rms(o, ret_g_ref[:, cols])
                gate = z_ref[rows, OFF_RG + hd * DV:OFF_RG + (hd + 1) * DV]
                y_ref[rows, cols] = (o * gate).astype(BF16)
                st = rstate_ref[hd] * cdec_ref[hd] + _dot(a["r_kd"][hd], v[:, cols])
                rstate_ref[hd] = st
                st_bf = st.astype(BF16)
                rstate_bd_ref[hd * hw:(hd + 1) * hw, cols] = st_bf[:hw]
                rstate_bd_ref[half + hd * hw:half + (hd + 1) * hw, cols] = st_bf[hw:]

        gv = a["g_v"]
        for p in range(HEADS // 2):
            ha, hb = 2 * p, 2 * p + 1
            st = gstate_ref[p]
            o_pair = (_dot(a["g_qe"][p], st.astype(BF16))
                      + _dot(jnp.concatenate([g_s[ha], g_s[hb]], axis=-1),
                             _block_diag2(gv[:, ha * DV:(ha + 1) * DV], gv[:, hb * DV:(hb + 1) * DV])))
            dec_col = a["g_dec"][p]
            st = st * jnp.concatenate([dec_col, dec_col], axis=-1)
            gstate_ref[p] = st
            kdt = a["g_kdt"][p]
            for hd in (ha, hb):
                j = hd - ha
                cols = slice(hd * DV, (hd + 1) * DV)
                upd = _dot(kdt[j * DK:(j + 1) * DK], gv[:, cols])
                gstate_ref[p, j * DK:(j + 1) * DK, j * DV:(j + 1) * DV] = (
                    st[j * DK:(j + 1) * DK, j * DV:(j + 1) * DV] + upd)
                o = _rms(o_pair[:, j * DV:(j + 1) * DV], gla_g_ref[:, cols])
                gate = z_ref[rows, OFF_GG + hd * DV:OFF_GG + (hd + 1) * DV]
                y_ref[rows, V_W + hd * DV:V_W + (hd + 1) * DV] = (o * gate).astype(BF16)

    n_sub_chunks = MIX_SUB // CHUNK
    assert tile == 2 * MIX_SUB and n_sub_chunks == N_GROUPS
    rows_a, rows_b = slice(0, MIX_SUB), slice(MIX_SUB, tile)
    pre_norm(rows_a)
    for g in range(N_GROUPS):
        in_projection(rows_a, g)
    pre_norm(rows_b)
    n_chunks = 2 * n_sub_chunks
    fillers = {0: [(in_projection, rows_b, 0), (in_projection, rows_b, 1)],
               1: [(in_projection, rows_b, 2), (in_projection, rows_b, 3)],
               4: [(out_projection, rows_a, 0)],
               5: [(out_projection, rows_a, 1)],
               6: [(out_projection, rows_a, 2)],
               7: [(out_projection, rows_a, 3)]}
    order = {0: [1], 1: [2], 2: [3, 4, 5], 3: [6, 7]}
    ready = {0: stage_a(0)}
    for c in range(n_chunks):
        for nxt in order.get(c, []):
            ready[nxt] = stage_a(nxt)
        stage_b(c, ready.pop(c))
        for fn, srows, g in fillers.get(c, []):
            fn(srows, g)
    post_norm(rows_a)
    for g in range(N_GROUPS):
        out_projection(rows_b, g)
    post_norm(rows_b)


def _mixer(x, layer, cos_t, sin_t, pre_g, w_in, w_a2, b_a2, ret_g, gla_g, w_out, post_g, tables):
    b, t, d = x.shape
    dmat, qdec, kdec, cdec = tables
    const2 = lambda bi, ti: (0, 0)
    lay2 = lambda bi, ti: (layer, 0, 0)
    tok = lambda bi, ti: (bi, ti, 0)
    return pl.pallas_call(
        _mixer_kernel,
        grid=(b, t // MIX_TILE),
        in_specs=[
            pl.BlockSpec((None, MIX_TILE, d), tok),
            pl.BlockSpec((None, MIX_TILE, QK_W // 2), tok),
            pl.BlockSpec((None, MIX_TILE, QK_W // 2), tok),
            pl.BlockSpec((None, 1, d), lay2),
            pl.BlockSpec((None, d, IN_W), lay2, pipeline_mode=pl.Buffered(1)),
            pl.BlockSpec((None, GATE_RANK_PAD, 2 * QK_W), lay2),
            pl.BlockSpec((None, 1, 2 * QK_W), lay2),
            pl.BlockSpec((None, 1, V_W), lay2),
            pl.BlockSpec((None, 1, V_W), lay2),
            pl.BlockSpec((None, 2 * V_W, d), lay2, pipeline_mode=pl.Buffered(1)),
            pl.BlockSpec((None, 1, d), lay2),
            pl.BlockSpec((HEADS * CHUNK, CHUNK), const2),
            pl.BlockSpec((CHUNK, QK_W), const2),
            pl.BlockSpec((CHUNK, QK_W), const2),
            pl.BlockSpec(memory_space=pltpu.SMEM),
        ],
        out_specs=pl.BlockSpec((None, MIX_TILE, d), tok),
        out_shape=jax.ShapeDtypeStruct(x.shape, F32),
        scratch_shapes=[
            pltpu.VMEM((MIX_TILE, d), BF16),
            pltpu.VMEM((MIX_TILE, IN_W), F32),
            pltpu.VMEM((MIX_TILE, 2 * QK_W), F32),
            pltpu.VMEM((MIX_TILE, 2 * V_W), BF16),
            pltpu.VMEM((HEADS, DK, DV), F32),
            pltpu.VMEM((QK_W, V_W), BF16),
            pltpu.VMEM((HEADS // 2, 2 * DK, 2 * DV), F32),
        ],
        compiler_params=pltpu.CompilerParams(
            dimension_semantics=("arbitrary", "arbitrary"), vmem_limit_bytes=VMEM_LIMIT),
        name="mixer",
    )(x, cos_t, sin_t, pre_g, w_in, w_a2, b_a2, ret_g, gla_g, w_out, post_g,
      dmat, qdec, kdec, cdec)


def _retention_tables():
    c = CHUNK
    log_gamma = jnp.log1p(-jnp.exp2(-5.0 - jnp.arange(HEADS, dtype=F32)))
    idx = jnp.arange(c, dtype=F32)
    rel = idx[:, None] - idx[None, :]
    causal = rel >= 0
    decay_in = jnp.where(causal[None],
                         jnp.exp(jnp.where(causal, rel, 0.0)[None] * log_gamma[:, None, None]), 0.0)
    q_dec = jnp.exp((idx + 1.0)[None, :] * log_gamma[:, None])
    k_dec = jnp.exp((c - 1.0 - idx)[None, :] * log_gamma[:, None])
    chunk_dec = jnp.exp(c * log_gamma)
    head_of_lane = (np.arange(QK_W) % (QK_W // 2)) // (DK // 2)
    dmat = decay_in.reshape(HEADS * c, c)
    qdec = q_dec.T[:, head_of_lane]
    kdec = k_dec.T[:, head_of_lane]
    return dmat, qdec, kdec, chunk_dec


def _rotary_cols(w):
    lead = w.shape[:-1]
    w = w.reshape(lead + (HEADS, 2, DK // 2))
    return jnp.swapaxes(w, -3, -2).reshape(lead + (QK_W,))


def _dup_cols(w):
    lead = w.shape[:-1]
    w = w.reshape(lead + (HEADS, 1, DK))
    return jnp.broadcast_to(w, lead + (HEADS, 2, DK)).reshape(lead + (2 * QK_W,))


def _relayout_w_in(w_in):
    sizes = (QK_W, QK_W, V_W, V_W, QK_W, QK_W, V_W, GATE_RANK, V_W)
    pts = np.cumsum(sizes)[:-1]
    rq, rk, rv, rg, gq, gk, gv, ga, gg = jnp.split(w_in, pts, axis=-1)
    ga = jnp.pad(ga, ((0, 0), (0, 0), (0, GATE_RANK_PAD - GATE_RANK)))
    return jnp.concatenate([ga, _dup_cols(gk), _dup_cols(gq) * QK_SCALE, gv,
                            _rotary_cols(rk) * QK_SCALE, _rotary_cols(rq), rv, gg, rg], axis=-1)


def _memkv_kernel(mem_ref, g_ref, w_ref, k_ref, v_ref):
    nb, m, d = mem_ref.shape
    x = mem_ref[...].reshape(nb * m, d)
    kv = _dot(_rms(x, g_ref[...]).astype(BF16), w_ref[...])
    for i in range(nb):
        k_ref[i] = kv[i * m:(i + 1) * m, :D_MODEL].T.astype(BF16)
        v_ref[i] = kv[i * m:(i + 1) * m, D_MODEL:].astype(BF16)


def _memkv(mem, layer, g, w_kv):
    b, m, d = mem.shape
    nb = MEMKV_BATCH
    lay2 = lambda i: (layer, 0, 0)
    return pl.pallas_call(
        _memkv_kernel,
        grid=(b // nb,),
        in_specs=[
            pl.BlockSpec((nb, m, d), lambda i: (i, 0, 0)),
            pl.BlockSpec((None, 1, d), lay2),
            pl.BlockSpec((None, d, 2 * d), lay2),
        ],
        out_specs=[pl.BlockSpec((nb, d, m), lambda i: (i, 0, 0)),
                   pl.BlockSpec((nb, m, d), lambda i: (i, 0, 0))],
        out_shape=[jax.ShapeDtypeStruct((b, d, m), BF16), jax.ShapeDtypeStruct((b, m, d), BF16)],
        compiler_params=pltpu.CompilerParams(
            dimension_semantics=("arbitrary",), vmem_limit_bytes=VMEM_LIMIT),
        name="memkv",
    )(mem, g, w_kv)


def _xattn_kernel(x_ref, kt_ref, v_ref, pre_g_ref, wq_ref, wo_ref, post_g_ref, o_ref, att_ref):
    for sub in range(XA_TILE // XA_SUB):
        rows = slice(sub * XA_SUB, (sub + 1) * XA_SUB)
        x = x_ref[rows, :]
        h = _rms(x, pre_g_ref[...]).astype(BF16)
        q = _dot(h, wq_ref[...])
        for hd in range(XA_HEADS):
            cols = slice(hd * XA_DH, (hd + 1) * XA_DH)
            s = _dot(q[:, cols].astype(BF16), kt_ref[cols, :])
            p = jnp.exp(s - jnp.max(s, axis=-1, keepdims=True))
            o = _dot(p.astype(BF16), v_ref[:, cols]) / jnp.sum(p, axis=-1, keepdims=True)
            att_ref[rows, cols] = o.astype(BF16)
        out = _dot(att_ref[rows, :], wo_ref[...])
        o_ref[rows, :] = x + _rms(out, post_g_ref[...])


def _xattn(x, layer, kt, v, pre_g, wq, wo, post_g):
    b, t, d = x.shape
    m = v.shape[1]
    lay2 = lambda bi, ti: (layer, 0, 0)
    tok = lambda bi, ti: (bi, ti, 0)
    mem = lambda bi, ti: (bi, 0, 0)
    return pl.pallas_call(
        _xattn_kernel,
        grid=(b, t // XA_TILE),
        in_specs=[
            pl.BlockSpec((None, XA_TILE, d), tok),
            pl.BlockSpec((None, d, m), mem),
            pl.BlockSpec((None, m, d), mem),
            pl.BlockSpec((None, 1, d), lay2),
            pl.BlockSpec((None, d, d), lay2),
            pl.BlockSpec((None, d, d), lay2),
            pl.BlockSpec((None, 1, d), lay2),
        ],
        out_specs=pl.BlockSpec((None, XA_TILE, d), tok),
        out_shape=jax.ShapeDtypeStruct(x.shape, F32),
        scratch_shapes=[pltpu.VMEM((XA_TILE, d), BF16)],
        compiler_params=pltpu.CompilerParams(
            dimension_semantics=("arbitrary", "arbitrary"), vmem_limit_bytes=VMEM_LIMIT),
        name="xattn",
    )(x, kt, v, pre_g, wq, wo, post_g)


def kernel(x, mem, positions, ffn1_pre_g, ffn1_w_gate, ffn1_w_up, ffn1_w_down, ffn1_post_g, mix_pre_g, w_in, w_a2, b_a2, ret_norm_g, gla_norm_g, w_out, mix_post_g, xa_pre_g, xa_mem_g, xa_w_q, xa_w_kv, xa_w_o, xa_post_g, ffn2_pre_g, ffn2_w_gate, ffn2_w_up, ffn2_w_down, ffn2_post_g):
    b, t, d = x.shape
    depth = w_in.shape[0]
    assert d == D_MODEL and t % MIX_TILE == 0 and t % XA_TILE == 0 and (b * t) % FFN_TILE == 0
    assert (b * t) % ROPE_TILE == 0

    cos_t, sin_t = _rope_tables(positions)
    cos_t = cos_t.reshape(b, t, QK_W // 2)
    sin_t = sin_t.reshape(b, t, QK_W // 2)
    tables = _retention_tables()

    row = lambda g: g.reshape(depth, 1, -1)
    bf = lambda w: w.astype(BF16)
    ffn1 = (row(ffn1_pre_g), bf(ffn1_w_gate), bf(ffn1_w_up), bf(ffn1_w_down), row(ffn1_post_g))
    ffn2 = (row(ffn2_pre_g), bf(ffn2_w_gate), bf(ffn2_w_up), bf(ffn2_w_down), row(ffn2_post_g))
    w_a2_p = bf(jnp.pad(_dup_cols(w_a2), ((0, 0), (0, GATE_RANK_PAD - GATE_RANK), (0, 0))))
    mix = (row(mix_pre_g), _relayout_w_in(bf(w_in)), w_a2_p, row(_dup_cols(b_a2)),
           row(ret_norm_g), row(gla_norm_g), bf(w_out), row(mix_post_g))
    xa_kv = (row(xa_mem_g), bf(xa_w_kv))
    xa = (row(xa_pre_g), bf(xa_w_q) * XA_SCALE, bf(xa_w_o), row(xa_post_g))

    for l in range(depth):
        x = _ffn(x.reshape(b * t, d), l, *ffn1).reshape(b, t, d)
        x = _mixer(x, l, cos_t, sin_t, *mix, tables)
        k_mem, v_mem = _memkv(mem, l, *xa_kv)
        x = _xattn(x, l, k_mem, v_mem, *xa)
        x = _ffn(x.reshape(b * t, d), l, *ffn2).reshape(b, t, d)
    return x
```

```python
import numpy as np
import jax
import jax.numpy as jnp
from jax import lax
from jax.experimental import pallas as pl
from jax.experimental.pallas import tpu as pltpu

D_MODEL = 1024
D_FF = 2816
EPS = 1e-6
ROPE_BASE = 10000.0

LANES = 128
HEADS = 4
DK = 64
DV = 128
QK_W = HEADS * DK
V_W = HEADS * DV
CHUNK = 128
GLA_SUB = 32
N_SUB = CHUNK // GLA_SUB
GATE_RANK = 16
GATE_RANK_PAD = 128
GATE_TAU = 16.0
QK_SCALE = DK ** -0.5

XA_HEADS = 4
XA_DH = D_MODEL // XA_HEADS
XA_SCALE = XA_DH ** -0.5
assert XA_SCALE == 2.0 ** -4 and QK_SCALE == 2.0 ** -3

OFF_GA = 0
OFF_GK = OFF_GA + GATE_RANK_PAD
OFF_GQ = OFF_GK + 2 * QK_W
OFF_GV = OFF_GQ + 2 * QK_W
OFF_RK = OFF_GV + V_W
OFF_RQ = OFF_RK + QK_W
OFF_RV = OFF_RQ + QK_W
OFF_GG = OFF_RV + V_W
OFF_RG = OFF_GG + V_W
IN_W = OFF_RG + V_W
N_GROUPS = 4
IN_GROUPS = (0, OFF_GV, OFF_RV, OFF_RG, IN_W)
OUT_GROUP = D_MODEL // N_GROUPS
ROPE_GROUP = 1
assert IN_GROUPS[ROPE_GROUP] <= OFF_RK and OFF_RQ + QK_W <= IN_GROUPS[ROPE_GROUP + 1]
LOG2E = 1.4426950408889634

FFN_TILE = 1024
FFN_SUB = 512
FFN_CHUNK = 256
MIX_TILE = 1024
MIX_SUB = 512
XA_TILE = 1024
XA_SUB = 512
ROPE_TILE = 1024
MEMKV_BATCH = 4
VMEM_LIMIT = 56 * 1024 * 1024

F32 = jnp.float32
BF16 = jnp.bfloat16


def _rms(x, g):
    return x * lax.rsqrt(jnp.mean(x * x, axis=-1, keepdims=True) + EPS) * g


def _unit_rows(x):
    n = x.shape[-1]
    return x * lax.rsqrt(jnp.sum(x * x, axis=-1, keepdims=True) + n * EPS)


def _silu(x):
    return x * (1.0 / (1.0 + jnp.exp(-x)))


def _dot(a, b):
    return jnp.dot(a, b, preferred_element_type=F32)


def _dot_nt(a, b):
    return lax.dot_general(a, b, (((1,), (1,)), ((), ())), preferred_element_type=F32)


def _block_diag2(a, b):
    za = jnp.zeros(a.shape, a.dtype)
    return jnp.concatenate(
        [jnp.concatenate([a, za], axis=-1), jnp.concatenate([za, b], axis=-1)], axis=0)


def _ffn_kernel(x_ref, pre_g_ref, wg_ref, wu_ref, wd_ref, post_g_ref, o_ref, u_ref):
    for s in range(FFN_TILE // FFN_SUB):
        rows = slice(s * FFN_SUB, (s + 1) * FFN_SUB)
        x = x_ref[rows, :]
        h = _rms(x, pre_g_ref[...]).astype(BF16)
        for lo in range(0, D_FF, FFN_CHUNK):
            a = _dot(h, wg_ref[:, lo:lo + FFN_CHUNK])
            b = _dot(h, wu_ref[:, lo:lo + FFN_CHUNK])
            u_ref[rows, lo:lo + FFN_CHUNK] = (_silu(a) * b).astype(BF16)
        acc = _dot(u_ref[rows, :], wd_ref[...])
        o_ref[rows, :] = x + _rms(acc, 0.5 * post_g_ref[...])


def _ffn(x2d, layer, pre_g, wg, wu, wd, post_g):
    n = x2d.shape[0]
    lay2 = lambda i: (layer, 0, 0)
    return pl.pallas_call(
        _ffn_kernel,
        grid=(n // FFN_TILE,),
        in_specs=[
            pl.BlockSpec((FFN_TILE, D_MODEL), lambda i: (i, 0)),
            pl.BlockSpec((None, 1, D_MODEL), lay2),
            pl.BlockSpec((None, D_MODEL, D_FF), lay2, pipeline_mode=pl.Buffered(1)),
            pl.BlockSpec((None, D_MODEL, D_FF), lay2, pipeline_mode=pl.Buffered(1)),
            pl.BlockSpec((None, D_FF, D_MODEL), lay2, pipeline_mode=pl.Buffered(1)),
            pl.BlockSpec((None, 1, D_MODEL), lay2),
        ],
        out_specs=pl.BlockSpec((FFN_TILE, D_MODEL), lambda i: (i, 0)),
        out_shape=jax.ShapeDtypeStruct(x2d.shape, F32),
        scratch_shapes=[pltpu.VMEM((FFN_TILE, D_FF), BF16)],
        compiler_params=pltpu.CompilerParams(
            dimension_semantics=("arbitrary",), vmem_limit_bytes=VMEM_LIMIT),
        name="ffn",
    )(x2d, pre_g, wg, wu, wd, post_g)


def _rope_kernel(pos_ref, freq_ref, cos_ref, sin_ref):
    ang = freq_ref[...] * pos_ref[...]
    cos_ref[...] = jnp.concatenate([jnp.cos(ang)] * HEADS, axis=0).T
    sin_ref[...] = jnp.concatenate([jnp.sin(ang)] * HEADS, axis=0).T


def _rope_tables(positions):
    b, t = positions.shape
    n = b * t
    pos = positions.astype(F32).reshape(n // ROPE_TILE, 1, ROPE_TILE)
    inv_freq = ROPE_BASE ** (-jnp.arange(0, DK, 2, dtype=F32) / DK)
    freq = inv_freq.reshape(DK // 2, 1)
    out = jax.ShapeDtypeStruct((n, QK_W // 2), F32)
    return pl.pallas_call(
        _rope_kernel,
        grid=(n // ROPE_TILE,),
        in_specs=[
            pl.BlockSpec((None, 1, ROPE_TILE), lambda i: (i, 0, 0)),
            pl.BlockSpec((DK // 2, 1), lambda i: (0, 0)),
        ],
        out_specs=[pl.BlockSpec((ROPE_TILE, QK_W // 2), lambda i: (i, 0))] * 2,
        out_shape=[out, out],
        compiler_params=pltpu.CompilerParams(dimension_semantics=("arbitrary",)),
        name="rope_tables",
    )(pos, freq)


def _mixer_kernel(x_ref, cos_ref, sin_ref, pre_g_ref, w_in_ref, w_a2_ref, b_a2_ref,
                  ret_g_ref, gla_g_ref, w_out_ref, post_g_ref,
                  dmat_ref, qdec_ref, kdec_ref, cdec_ref,
                  o_ref,
                  h_ref, z_ref, la_ref, y_ref, rstate_ref, rstate_bd_ref, gstate_ref):
    tile = x_ref.shape[0]

    @pl.when(pl.program_id(1) == 0)
    def _():
        rstate_ref[...] = jnp.zeros(rstate_ref.shape, F32)
        rstate_bd_ref[...] = jnp.zeros(rstate_bd_ref.shape, BF16)
        gstate_ref[...] = jnp.zeros(gstate_ref.shape, F32)

    def pre_norm(srows):
        h_ref[srows, :] = _rms(x_ref[srows, :], pre_g_ref[...]).astype(BF16)

    def in_projection(srows, g):
        lo, hi = IN_GROUPS[g], IN_GROUPS[g + 1]
        z_ref[srows, lo:hi] = _dot(h_ref[srows, :], w_in_ref[:, lo:hi])
        if g == 0:
            ga = z_ref[srows, OFF_GA:OFF_GA + GATE_RANK_PAD].astype(BF16)
            pre = _dot(ga, w_a2_ref[...]) + b_a2_ref[...]
            la_ref[srows, :] = ((jnp.minimum(pre, 0.0) - jnp.log1p(jnp.exp(-jnp.abs(pre))))
                                * (LOG2E / GATE_TAU))
        if g == ROPE_GROUP:
            cosv, sinv = cos_ref[srows, :], sin_ref[srows, :]
            for off in (OFF_RK, OFF_RQ):
                x1 = z_ref[srows, off:off + QK_W // 2]
                x2 = z_ref[srows, off + QK_W // 2:off + QK_W]
                z_ref[srows, off:off + QK_W // 2] = x1 * cosv - x2 * sinv
                z_ref[srows, off + QK_W // 2:off + QK_W] = x1 * sinv + x2 * cosv
        if g == N_GROUPS - 1:
            gain = jnp.concatenate([gla_g_ref[...], ret_g_ref[...]], axis=-1) * (DV ** 0.5)
            z_ref[srows, OFF_GG:IN_W] = _silu(z_ref[srows, OFF_GG:IN_W]) * gain

    def out_projection(srows, g):
        cols = slice(g * OUT_GROUP, (g + 1) * OUT_GROUP)
        o_ref[srows, cols] = _dot(y_ref[srows, :], w_out_ref[:, cols])

    def post_norm(srows):
        o_ref[srows, :] = x_ref[srows, :] + _rms(o_ref[srows, :], post_g_ref[...])

    half = QK_W // 2
    hw = DK // 2
    lane_qk = lax.broadcasted_iota(jnp.int32, (1, QK_W), 1)
    ret_masks = [(lane_qk % half) // hw == hd for hd in range(HEADS)]

    ri = lax.broadcasted_iota(jnp.int32, (CHUNK, CHUNK), 0)
    ci = lax.broadcasted_iota(jnp.int32, (CHUNK, CHUNK), 1)
    causal = ri >= ci
    tri_bf = jnp.where(causal, 1.0, 0.0).astype(BF16)
    tri2_bf = jnp.concatenate([tri_bf, tri_bf], axis=1)
    lane_v = lax.broadcasted_iota(jnp.int32, (1, V_W), 1)
    dup_half = (lane_v % LANES) // DK
    first_half = lax.broadcasted_iota(jnp.int32, (1, LANES), 1) < DK

    def natural_pair(a, p):
        return jnp.where(first_half, a[:, 2 * p * LANES:(2 * p + 1) * LANES],
                         a[:, (2 * p + 1) * LANES:(2 * p + 2) * LANES])

    blk = lambda a, rb: a[rb * GLA_SUB:(rb + 1) * GLA_SUB]

    def stage_a(c):
        rows = slice(c * CHUNK, (c + 1) * CHUNK)
        out = {}

        q = z_ref[rows, OFF_RQ:OFF_RQ + QK_W]
        k = z_ref[rows, OFF_RK:OFF_RK + QK_W]
        v = z_ref[rows, OFF_RV:OFF_RV + V_W].astype(BF16)
        qstack = jnp.concatenate([jnp.where(m, q, 0.0) for m in ret_masks], axis=0).astype(BF16)
        s_all = (_dot_nt(qstack, k.astype(BF16)) * dmat_ref[...]).astype(BF16)
        kdt = (k * kdec_ref[...]).T
        out["r_v"] = v
        out["r_s"] = s_all
        out["r_qd"] = (q * qdec_ref[...]).astype(BF16)
        out["r_kd"] = [jnp.concatenate([kdt[hd * hw:(hd + 1) * hw],
                                        kdt[half + hd * hw:half + (hd + 1) * hw]], axis=0).astype(BF16)
                       for hd in range(HEADS)]

        la = la_ref[rows, :]
        a1 = la.astype(BF16)
        a2 = (la - a1.astype(F32)).astype(BF16)
        cum = _dot(tri2_bf, jnp.concatenate([a1, a2], axis=0))
        gq = z_ref[rows, OFF_GQ:OFF_GQ + 2 * QK_W]
        gk = z_ref[rows, OFF_GK:OFF_GK + 2 * QK_W]
        gv = z_ref[rows, OFF_GV:OFF_GV + V_W].astype(BF16)

        mid = GLA_SUB // 2 - 1
        refs = [cum[s * GLA_SUB + mid:s * GLA_SUB + mid + 1, :] for s in range(N_SUB)]
        last = cum[CHUNK - 1:CHUNK, :]
        d = cum - jnp.concatenate([jnp.broadcast_to(r, (GLA_SUB, V_W)) for r in refs], axis=0)
        e_own = jnp.exp2(d)
        gq_e = gq * e_own
        gk_e = gk / e_own
        zero_blk = jnp.zeros((GLA_SUB, V_W), BF16)
        q_sets, k_sets = [], []
        for t2 in range(N_SUB // 2):
            q_rows, k_rows = [], []
            for rb in range(N_SUB):
                s_lo, s_hi = 2 * t2, 2 * t2 + 1
                if s_lo > rb:
                    q_rows.append(zero_blk)
                else:
                    f_lo = jnp.exp2(refs[rb] - refs[s_lo])
                    f_hi = jnp.exp2(refs[rb] - refs[s_hi]) if s_hi <= rb else jnp.zeros_like(f_lo)
                    q_rows.append((blk(gq_e, rb) * jnp.where(dup_half == 0, f_lo, f_hi)).astype(BF16))
                if rb // 2 == t2:
                    k_rows.append(jnp.where(dup_half == rb % 2, blk(gk_e, rb), 0.0).astype(BF16))
                else:
                    k_rows.append(zero_blk)
            q_sets.append(jnp.concatenate(q_rows, axis=0))
            k_sets.append(jnp.concatenate(k_rows, axis=0))

        e_last = jnp.exp2(last)
        out["g_v"] = gv
        out["g_q"] = [jnp.concatenate([qs[:, hd * LANES:(hd + 1) * LANES] for qs in q_sets], axis=-1)
                      for hd in range(HEADS)]
        out["g_k"] = [jnp.concatenate([ks[:, hd * LANES:(hd + 1) * LANES] for ks in k_sets], axis=-1)
                      for hd in range(HEADS)]
        out["g_qe"], out["g_kdt"], out["g_dec"] = [], [], []
        for p in range(HEADS // 2):
            gq_n = natural_pair(gq_e, p)
            gk_n = natural_pair(gk_e, p)
            out["g_qe"].append(jnp.concatenate(
                [blk(gq_n, rb) * natural_pair(jnp.exp2(refs[rb]), p) for rb in range(N_SUB)],
                axis=0).astype(BF16))
            kd = jnp.concatenate(
                [blk(gk_n, rb) * natural_pair(jnp.exp2(last - refs[rb]), p) for rb in range(N_SUB)],
                axis=0)
            out["g_kdt"].append(kd.T.astype(BF16))
            out["g_dec"].append(jnp.broadcast_to(natural_pair(e_last, p), (CHUNK, LANES)).T)
        return out

    def stage_b(c, a):
        rows = slice(c * CHUNK, (c + 1) * CHUNK)
        g_s = [jnp.where(causal, _dot_nt(a["g_q"][hd], a["g_k"][hd]), 0.0).astype(BF16)
               for hd in range(HEADS)]

        v, s_all = a["r_v"], a["r_s"]
        o_inter = _dot(a["r_qd"], rstate_bd_ref[...])
        for p in range(HEADS // 2):
            ha, hb = 2 * p, 2 * p + 1
            s_pair = jnp.concatenate([s_all[ha * CHUNK:(ha + 1) * CHUNK],
                                      s_all[hb * CHUNK:(hb + 1) * CHUNK]], axis=-1)
            o_pair = _dot(s_pair, _block_diag2(v[:, ha * DV:(ha + 1) * DV], v[:, hb * DV:(hb + 1) * DV]))
            for hd in (ha, hb):
                cols = slice(hd * DV, (hd + 1) * DV)
                o = o_pair[:, (hd - ha) * DV:(hd - ha + 1) * DV] + o_inter[:, cols]
                gate = z_ref[rows, OFF_RG + hd * DV:OFF_RG + (hd + 1) * DV]
                y_ref[rows, cols] = (_unit_rows(o) * gate).astype(BF16)
                st = rstate_ref[hd] * cdec_ref[hd] + _dot(a["r_kd"][hd], v[:, cols])
                rstate_ref[hd] = st
                st_bf = st.astype(BF16)
                rstate_bd_ref[hd * hw:(hd + 1) * hw, cols] = st_bf[:hw]
                rstate_bd_ref[half + hd * hw:half + (hd + 1) * hw, cols] = st_bf[hw:]

        gv = a["g_v"]
        for p in range(HEADS // 2):
            ha, hb = 2 * p, 2 * p + 1
            st = gstate_ref[p]
            o_pair = (_dot(a["g_qe"][p], st.astype(BF16))
                      + _dot(jnp.concatenate([g_s[ha], g_s[hb]], axis=-1),
                             _block_diag2(gv[:, ha * DV:(ha + 1) * DV], gv[:, hb * DV:(hb + 1) * DV])))
            dec_col = a["g_dec"][p]
            st = st * jnp.concatenate([dec_col, dec_col], axis=-1)
            gstate_ref[p] = st
            kdt = a["g_kdt"][p]
            for hd in (ha, hb):
                j = hd - ha
                cols = slice(hd * DV, (hd + 1) * DV)
                upd = _dot(kdt[j * DK:(j + 1) * DK], gv[:, cols])
                gstate_ref[p, j * DK:(j + 1) * DK, j * DV:(j + 1) * DV] = (
                    st[j * DK:(j + 1) * DK, j * DV:(j + 1) * DV] + upd)
                gate = z_ref[rows, OFF_GG + hd * DV:OFF_GG + (hd + 1) * DV]
                y_ref[rows, V_W + hd * DV:V_W + (hd + 1) * DV] = (
                    _unit_rows(o_pair[:, j * DV:(j + 1) * DV]) * gate).astype(BF16)

    n_sub_chunks = MIX_SUB // CHUNK
    assert tile == 2 * MIX_SUB and n_sub_chunks == N_GROUPS
    rows_a, rows_b = slice(0, MIX_SUB), slice(MIX_SUB, tile)
    pre_norm(rows_a)
    for g in range(N_GROUPS):
        in_projection(rows_a, g)
    pre_norm(rows_b)
    n_chunks = 2 * n_sub_chunks
    fillers = {0: [(in_projection, rows_b, 0), (in_projection, rows_b, 1)],
               1: [(in_projection, rows_b, 2), (in_projection, rows_b, 3)],
               4: [(out_projection, rows_a, 0)],
               5: [(out_projection, rows_a, 1)],
               6: [(out_projection, rows_a, 2)],
               7: [(out_projection, rows_a, 3)]}
    order = {0: [1], 1: [2], 2: [3, 4, 5], 3: [6, 7]}
    ready = {0: stage_a(0)}
    for c in range(n_chunks):
        for nxt in order.get(c, []):
            ready[nxt] = stage_a(nxt)
        stage_b(c, ready.pop(c))
        for fn, srows, g in fillers.get(c, []):
            fn(srows, g)
    post_norm(rows_a)
    for g in range(N_GROUPS):
        out_projection(rows_b, g)
    post_norm(rows_b)


def _mixer(x, layer, cos_t, sin_t, pre_g, w_in, w_a2, b_a2, ret_g, gla_g, w_out, post_g, tables):
    b, t, d = x.shape
    dmat, qdec, kdec, cdec = tables
    const2 = lambda bi, ti: (0, 0)
    lay2 = lambda bi, ti: (layer, 0, 0)
    tok = lambda bi, ti: (bi, ti, 0)
    return pl.pallas_call(
        _mixer_kernel,
        grid=(b, t // MIX_TILE),
        in_specs=[
            pl.BlockSpec((None, MIX_TILE, d), tok),
            pl.BlockSpec((None, MIX_TILE, QK_W // 2), tok),
            pl.BlockSpec((None, MIX_TILE, QK_W // 2), tok),
            pl.BlockSpec((None, 1, d), lay2),
            pl.BlockSpec((None, d, IN_W), lay2, pipeline_mode=pl.Buffered(1)),
            pl.BlockSpec((None, GATE_RANK_PAD, 2 * QK_W), lay2),
            pl.BlockSpec((None, 1, 2 * QK_W), lay2),
            pl.BlockSpec((None, 1, V_W), lay2),
            pl.BlockSpec((None, 1, V_W), lay2),
            pl.BlockSpec((None, 2 * V_W, d), lay2, pipeline_mode=pl.Buffered(1)),
            pl.BlockSpec((None, 1, d), lay2),
            pl.BlockSpec((HEADS * CHUNK, CHUNK), const2),
            pl.BlockSpec((CHUNK, QK_W), const2),
            pl.BlockSpec((CHUNK, QK_W), const2),
            pl.BlockSpec(memory_space=pltpu.SMEM),
        ],
        out_specs=pl.BlockSpec((None, MIX_TILE, d), tok),
        out_shape=jax.ShapeDtypeStruct(x.shape, F32),
        scratch_shapes=[
            pltpu.VMEM((MIX_TILE, d), BF16),
            pltpu.VMEM((MIX_TILE, IN_W), F32),
            pltpu.VMEM((MIX_TILE, 2 * QK_W), F32),
            pltpu.VMEM((MIX_TILE, 2 * V_W), BF16),
            pltpu.VMEM((HEADS, DK, DV), F32),
            pltpu.VMEM((QK_W, V_W), BF16),
            pltpu.VMEM((HEADS // 2, 2 * DK, 2 * DV), F32),
        ],
        compiler_params=pltpu.CompilerParams(
            dimension_semantics=("arbitrary", "arbitrary"), vmem_limit_bytes=VMEM_LIMIT),
        name="mixer",
    )(x, cos_t, sin_t, pre_g, w_in, w_a2, b_a2, ret_g, gla_g, w_out, post_g,
      dmat, qdec, kdec, cdec)


def _retention_tables():
    c = CHUNK
    log_gamma = jnp.log1p(-jnp.exp2(-5.0 - jnp.arange(HEADS, dtype=F32)))
    idx = jnp.arange(c, dtype=F32)
    rel = idx[:, None] - idx[None, :]
    causal = rel >= 0
    decay_in = jnp.where(causal[None],
                         jnp.exp(jnp.where(causal, rel, 0.0)[None] * log_gamma[:, None, None]), 0.0)
    q_dec = jnp.exp((idx + 1.0)[None, :] * log_gamma[:, None])
    k_dec = jnp.exp((c - 1.0 - idx)[None, :] * log_gamma[:, None])
    chunk_dec = jnp.exp(c * log_gamma)
    head_of_lane = (np.arange(QK_W) % (QK_W // 2)) // (DK // 2)
    dmat = decay_in.reshape(HEADS * c, c)
    qdec = q_dec.T[:, head_of_lane]
    kdec = k_dec.T[:, head_of_lane]
    return dmat, qdec, kdec, chunk_dec


def _rotary_cols(w):
    lead = w.shape[:-1]
    w = w.reshape(lead + (HEADS, 2, DK // 2))
    return jnp.swapaxes(w, -3, -2).reshape(lead + (QK_W,))


def _dup_cols(w):
    lead = w.shape[:-1]
    w = w.reshape(lead + (HEADS, 1, DK))
    return jnp.broadcast_to(w, lead + (HEADS, 2, DK)).reshape(lead + (2 * QK_W,))


def _relayout_w_in(w_in):
    sizes = (QK_W, QK_W, V_W, V_W, QK_W, QK_W, V_W, GATE_RANK, V_W)
    pts = np.cumsum(sizes)[:-1]
    rq, rk, rv, rg, gq, gk, gv, ga, gg = jnp.split(w_in, pts, axis=-1)
    ga = jnp.pad(ga, ((0, 0), (0, 0), (0, GATE_RANK_PAD - GATE_RANK)))
    return jnp.concatenate([ga, _dup_cols(gk), _dup_cols(gq) * QK_SCALE, gv,
                            _rotary_cols(rk) * QK_SCALE, _rotary_cols(rq), rv, gg, rg], axis=-1)


def _memkv_kernel(mem_ref, g_ref, w_ref, k_ref, v_ref):
    nb, m, d = mem_ref.shape
    x = mem_ref[...].reshape(nb * m, d)
    kv = _dot(_rms(x, g_ref[...]).astype(BF16), w_ref[...])
    for i in range(nb):
        k_ref[i] = kv[i * m:(i + 1) * m, :D_MODEL].T.astype(BF16)
        v_ref[i] = kv[i * m:(i + 1) * m, D_MODEL:].astype(BF16)


def _memkv(mem, layer, g, w_kv):
    b, m, d = mem.shape
    nb = MEMKV_BATCH
    lay2 = lambda i: (layer, 0, 0)
    return pl.pallas_call(
        _memkv_kernel,
        grid=(b // nb,),
        in_specs=[
            pl.BlockSpec((nb, m, d), lambda i: (i, 0, 0)),
            pl.BlockSpec((None, 1, d), lay2),
            pl.BlockSpec((None, d, 2 * d), lay2),
        ],
        out_specs=[pl.BlockSpec((nb, d, m), lambda i: (i, 0, 0)),
                   pl.BlockSpec((nb, m, d), lambda i: (i, 0, 0))],
        out_shape=[jax.ShapeDtypeStruct((b, d, m), BF16), jax.ShapeDtypeStruct((b, m, d), BF16)],
        compiler_params=pltpu.CompilerParams(
            dimension_semantics=("arbitrary",), vmem_limit_bytes=VMEM_LIMIT),
        name="memkv",
    )(mem, g, w_kv)


def _xattn_kernel(x_ref, kt_ref, v_ref, pre_g_ref, wq_ref, wo_ref, post_g_ref, o_ref, att_ref):
    for sub in range(XA_TILE // XA_SUB):
        rows = slice(sub * XA_SUB, (sub + 1) * XA_SUB)
        x = x_ref[rows, :]
        h = _rms(x, pre_g_ref[...]).astype(BF16)
        q = _dot(h, wq_ref[...])
        for hd in range(XA_HEADS):
            cols = slice(hd * XA_DH, (hd + 1) * XA_DH)
            s = _dot(q[:, cols].astype(BF16), kt_ref[cols, :])
            p = jnp.exp(s - jnp.max(s, axis=-1, keepdims=True))
            o = _dot(p.astype(BF16), v_ref[:, cols]) / jnp.sum(p, axis=-1, keepdims=True)
            att_ref[rows, cols] = o.astype(BF16)
        out = _dot(att_ref[rows, :], wo_ref[...])
        o_ref[rows, :] = x + _rms(out, post_g_ref[...])


def _xattn(x, layer, kt, v, pre_g, wq, wo, post_g):
    b, t, d = x.shape
    m = v.shape[1]
    lay2 = lambda bi, ti: (layer, 0, 0)
    tok = lambda bi, ti: (bi, ti, 0)
    mem = lambda bi, ti: (bi, 0, 0)
    return pl.pallas_call(
        _xattn_kernel,
        grid=(b, t // XA_TILE),
        in_specs=[
            pl.BlockSpec((None, XA_TILE, d), tok),
            pl.BlockSpec((None, d, m), mem),
            pl.BlockSpec((None, m, d), mem),
            pl.BlockSpec((None, 1, d), lay2),
            pl.BlockSpec((None, d, d), lay2),
            pl.BlockSpec((None, d, d), lay2),
            pl.BlockSpec((None, 1, d), lay2),
        ],
        out_specs=pl.BlockSpec((None, XA_TILE, d), tok),
        out_shape=jax.ShapeDtypeStruct(x.shape, F32),
        scratch_shapes=[pltpu.VMEM((XA_TILE, d), BF16)],
        compiler_params=pltpu.CompilerParams(
            dimension_semantics=("arbitrary", "arbitrary"), vmem_limit_bytes=VMEM_LIMIT),
        name="xattn",
    )(x, kt, v, pre_g, wq, wo, post_g)


def kernel(x, mem, positions, ffn1_pre_g, ffn1_w_gate, ffn1_w_up, ffn1_w_down, ffn1_post_g, mix_pre_g, w_in, w_a2, b_a2, ret_norm_g, gla_norm_g, w_out, mix_post_g, xa_pre_g, xa_mem_g, xa_w_q, xa_w_kv, xa_w_o, xa_post_g, ffn2_pre_g, ffn2_w_gate, ffn2_w_up, ffn2_w_down, ffn2_post_g):
    b, t, d = x.shape
    depth = w_in.shape[0]
    assert d == D_MODEL and t % MIX_TILE == 0 and t % XA_TILE == 0 and (b * t) % FFN_TILE == 0
    assert (b * t) % ROPE_TILE == 0 and b % MEMKV_BATCH == 0

    cos_t, sin_t = _rope_tables(positions)
    cos_t = cos_t.reshape(b, t, QK_W // 2)
    sin_t = sin_t.reshape(b, t, QK_W // 2)
    tables = _retention_tables()

    row = lambda g: g.reshape(depth, 1, -1)
    bf = lambda w: w.astype(BF16)
    ffn1 = (row(ffn1_pre_g), bf(ffn1_w_gate), bf(ffn1_w_up), bf(ffn1_w_down), row(ffn1_post_g))
    ffn2 = (row(ffn2_pre_g), bf(ffn2_w_gate), bf(ffn2_w_up), bf(ffn2_w_down), row(ffn2_post_g))
    w_a2_p = bf(jnp.pad(_dup_cols(w_a2), ((0, 0), (0, GATE_RANK_PAD - GATE_RANK), (0, 0))))
    mix = (row(mix_pre_g), _relayout_w_in(bf(w_in)), w_a2_p, row(_dup_cols(b_a2)),
           row(ret_norm_g), row(gla_norm_g), bf(w_out), row(mix_post_g))
    xa_kv = (row(xa_mem_g), bf(xa_w_kv))
    xa = (row(xa_pre_g), bf(xa_w_q) * XA_SCALE, bf(xa_w_o), row(xa_post_g))

    for l in range(depth):
        x = _ffn(x.reshape(b * t, d), l, *ffn1).reshape(b, t, d)
        x = _mixer(x, l, cos_t, sin_t, *mix, tables)
        k_mem, v_mem = _memkv(mem, l, *xa_kv)
        x = _xattn(x, l, k_mem, v_mem, *xa)
        x = _ffn(x.reshape(b * t, d), l, *ffn2).reshape(b, t, d)
    return x
```

```python
import numpy as np
import jax
import jax.numpy as jnp
from jax import lax
from jax.experimental import pallas as pl
from jax.experimental.pallas import tpu as pltpu

D_MODEL = 1024
D_FF = 2816
EPS = 1e-6
ROPE_BASE = 10000.0

LANES = 128
HEADS = 4
DK = 64
DV = 128
QK_W = HEADS * DK
V_W = HEADS * DV
CHUNK = 128
GLA_SUB = 32
N_SUB = CHUNK // GLA_SUB
GATE_RANK = 16
GATE_RANK_PAD = 128
GATE_TAU = 16.0
QK_SCALE = DK ** -0.5

XA_HEADS = 4
XA_DH = D_MODEL // XA_HEADS
XA_SCALE = XA_DH ** -0.5
assert XA_SCALE == 2.0 ** -4 and QK_SCALE == 2.0 ** -3

OFF_GA = 0
OFF_GK = OFF_GA + GATE_RANK_PAD
OFF_GQ = OFF_GK + 2 * QK_W
OFF_GV = OFF_GQ + 2 * QK_W
OFF_RK = OFF_GV + V_W
OFF_RQ = OFF_RK + QK_W
OFF_RV = OFF_RQ + QK_W
OFF_GG = OFF_RV + V_W
OFF_RG = OFF_GG + V_W
IN_W = OFF_RG + V_W
N_GROUPS = 4
IN_GROUPS = (0, OFF_GV, OFF_RV, OFF_RG, IN_W)
OUT_GROUP = D_MODEL // N_GROUPS
LOG2E = 1.4426950408889634

FFN_TILE = 1024
FFN_SUB = 512
FFN_CHUNK = 256
MIX_TILE = 1024
MIX_SUB_CHUNKS = (2, 3, 3)
XA_TILE = 1024
XA_SUB = 512
ROPE_TILE = 1024
MEMKV_BATCH = 4
VMEM_LIMIT = 56 * 1024 * 1024

F32 = jnp.float32
BF16 = jnp.bfloat16


def _rms(x, g):
    return x * lax.rsqrt(jnp.mean(x * x, axis=-1, keepdims=True) + EPS) * g


def _silu(x):
    return x * (1.0 / (1.0 + jnp.exp(-x)))


def _dot(a, b):
    return jnp.dot(a, b, preferred_element_type=F32)


def _dot_nt(a, b):
    return lax.dot_general(a, b, (((1,), (1,)), ((), ())), preferred_element_type=F32)


def _block_diag2(a, b):
    za = jnp.zeros(a.shape, a.dtype)
    return jnp.concatenate(
        [jnp.concatenate([a, za], axis=-1), jnp.concatenate([za, b], axis=-1)], axis=0)


def _ffn_kernel(x_ref, pre_g_ref, wg_ref, wu_ref, wd_ref, post_g_ref, o_ref, u_ref):
    for s in range(FFN_TILE // FFN_SUB):
        rows = slice(s * FFN_SUB, (s + 1) * FFN_SUB)
        x = x_ref[rows, :]
        h = _rms(x, pre_g_ref[...]).astype(BF16)
        for lo in range(0, D_FF, FFN_CHUNK):
            a = _dot(h, wg_ref[:, lo:lo + FFN_CHUNK])
            b = _dot(h, wu_ref[:, lo:lo + FFN_CHUNK])
            u_ref[rows, lo:lo + FFN_CHUNK] = (_silu(a) * b).astype(BF16)
        acc = _dot(u_ref[rows, :], wd_ref[...])
        o_ref[rows, :] = x + _rms(acc, 0.5 * post_g_ref[...])


def _ffn(x2d, layer, pre_g, wg, wu, wd, post_g):
    n = x2d.shape[0]
    lay2 = lambda i: (layer, 0, 0)
    return pl.pallas_call(
        _ffn_kernel,
        grid=(n // FFN_TILE,),
        in_specs=[
            pl.BlockSpec((FFN_TILE, D_MODEL), lambda i: (i, 0)),
            pl.BlockSpec((None, 1, D_MODEL), lay2),
            pl.BlockSpec((None, D_MODEL, D_FF), lay2, pipeline_mode=pl.Buffered(1)),
            pl.BlockSpec((None, D_MODEL, D_FF), lay2, pipeline_mode=pl.Buffered(1)),
            pl.BlockSpec((None, D_FF, D_MODEL), lay2, pipeline_mode=pl.Buffered(1)),
            pl.BlockSpec((None, 1, D_MODEL), lay2),
        ],
        out_specs=pl.BlockSpec((FFN_TILE, D_MODEL), lambda i: (i, 0)),
        out_shape=jax.ShapeDtypeStruct(x2d.shape, F32),
        scratch_shapes=[pltpu.VMEM((FFN_TILE, D_FF), BF16)],
        compiler_params=pltpu.CompilerParams(
            dimension_semantics=("arbitrary",), vmem_limit_bytes=VMEM_LIMIT),
        name="ffn",
    )(x2d, pre_g, wg, wu, wd, post_g)


def _rope_kernel(pos_ref, freq_ref, cos_ref, sin_ref):
    ang = freq_ref[...] * pos_ref[...]
    cos_ref[...] = jnp.concatenate([jnp.cos(ang)] * HEADS, axis=0).T
    sin_ref[...] = jnp.concatenate([jnp.sin(ang)] * HEADS, axis=0).T


def _rope_tables(positions):
    b, t = positions.shape
    n = b * t
    pos = positions.astype(F32).reshape(n // ROPE_TILE, 1, ROPE_TILE)
    inv_freq = ROPE_BASE ** (-jnp.arange(0, DK, 2, dtype=F32) / DK)
    freq = inv_freq.reshape(DK // 2, 1)
    out = jax.ShapeDtypeStruct((n, QK_W // 2), F32)
    return pl.pallas_call(
        _rope_kernel,
        grid=(n // ROPE_TILE,),
        in_specs=[
            pl.BlockSpec((None, 1, ROPE_TILE), lambda i: (i, 0, 0)),
            pl.BlockSpec((DK // 2, 1), lambda i: (0, 0)),
        ],
        out_specs=[pl.BlockSpec((ROPE_TILE, QK_W // 2), lambda i: (i, 0))] * 2,
        out_shape=[out, out],
        compiler_params=pltpu.CompilerParams(dimension_semantics=("arbitrary",)),
        name="rope_tables",
    )(pos, freq)


def _mixer_kernel(x_ref, cos_ref, sin_ref, pre_g_ref, w_in_ref, w_a2_ref, b_a2_ref,
                  ret_g_ref, gla_g_ref, w_out_ref, post_g_ref,
                  dmat_ref, qdec_ref, kdec_ref, cdec_ref,
                  o_ref,
                  h_ref, z_ref, la_ref, y_ref, rstate_ref, rstate_bd_ref, gstate_ref):
    tile = x_ref.shape[0]

    @pl.when(pl.program_id(1) == 0)
    def _():
        rstate_ref[...] = jnp.zeros(rstate_ref.shape, F32)
        rstate_bd_ref[...] = jnp.zeros(rstate_bd_ref.shape, BF16)
        gstate_ref[...] = jnp.zeros(gstate_ref.shape, F32)

    def pre_norm(srows):
        h_ref[srows, :] = _rms(x_ref[srows, :], pre_g_ref[...]).astype(BF16)

    def in_projection(srows, g):
        lo, hi = IN_GROUPS[g], IN_GROUPS[g + 1]
        z_ref[srows, lo:hi] = _dot(h_ref[srows, :], w_in_ref[:, lo:hi])
        if g == 0:
            ga = z_ref[srows, OFF_GA:OFF_GA + GATE_RANK_PAD].astype(BF16)
            pre = _dot(ga, w_a2_ref[...]) + b_a2_ref[...]
            la_ref[srows, :] = ((jnp.minimum(pre, 0.0) - jnp.log1p(jnp.exp(-jnp.abs(pre))))
                                * (LOG2E / GATE_TAU))
        if g == N_GROUPS - 1:
            z_ref[srows, OFF_GG:IN_W] = _silu(z_ref[srows, OFF_GG:IN_W])

    def out_projection(srows, g):
        cols = slice(g * OUT_GROUP, (g + 1) * OUT_GROUP)
        o_ref[srows, cols] = _dot(y_ref[srows, :], w_out_ref[:, cols])

    def post_norm(srows):
        o_ref[srows, :] = x_ref[srows, :] + _rms(o_ref[srows, :], post_g_ref[...])

    half = QK_W // 2
    hw = DK // 2
    lane_qk = lax.broadcasted_iota(jnp.int32, (1, QK_W), 1)
    ret_masks = [(lane_qk % half) // hw == hd for hd in range(HEADS)]

    ri = lax.broadcasted_iota(jnp.int32, (CHUNK, CHUNK), 0)
    ci = lax.broadcasted_iota(jnp.int32, (CHUNK, CHUNK), 1)
    causal = ri >= ci
    tri_bf = jnp.where(causal, 1.0, 0.0).astype(BF16)
    tri2_bf = jnp.concatenate([tri_bf, tri_bf], axis=1)
    lane_v = lax.broadcasted_iota(jnp.int32, (1, V_W), 1)
    dup_half = (lane_v % LANES) // DK
    first_half = lax.broadcasted_iota(jnp.int32, (1, LANES), 1) < DK

    def natural_pair(a, p):
        return jnp.where(first_half, a[:, 2 * p * LANES:(2 * p + 1) * LANES],
                         a[:, (2 * p + 1) * LANES:(2 * p + 2) * LANES])

    blk = lambda a, rb: a[rb * GLA_SUB:(rb + 1) * GLA_SUB]

    def stage_a(c):
        rows = slice(c * CHUNK, (c + 1) * CHUNK)
        out = {}

        cosv = cos_ref[rows, :]
        sinv = sin_ref[rows, :]

        def rope(off):
            x1 = z_ref[rows, off:off + half]
            x2 = z_ref[rows, off + half:off + QK_W]
            return jnp.concatenate([x1 * cosv - x2 * sinv, x1 * sinv + x2 * cosv], axis=-1)

        q = rope(OFF_RQ)
        k = rope(OFF_RK)
        v = z_ref[rows, OFF_RV:OFF_RV + V_W].astype(BF16)
        qstack = jnp.concatenate([jnp.where(m, q, 0.0) for m in ret_masks], axis=0).astype(BF16)
        s_all = (_dot_nt(qstack, k.astype(BF16)) * dmat_ref[...]).astype(BF16)
        kdt = (k * kdec_ref[...]).T
        out["r_v"] = v
        out["r_s"] = s_all
        out["r_qd"] = (q * qdec_ref[...]).astype(BF16)
        out["r_kd"] = [jnp.concatenate([kdt[hd * hw:(hd + 1) * hw],
                                        kdt[half + hd * hw:half + (hd + 1) * hw]], axis=0).astype(BF16)
                       for hd in range(HEADS)]

        la = la_ref[rows, :]
        a1 = la.astype(BF16)
        a2 = (la - a1.astype(F32)).astype(BF16)
        cum = _dot(tri2_bf, jnp.concatenate([a1, a2], axis=0))
        gq = z_ref[rows, OFF_GQ:OFF_GQ + 2 * QK_W]
        gk = z_ref[rows, OFF_GK:OFF_GK + 2 * QK_W]
        gv = z_ref[rows, OFF_GV:OFF_GV + V_W].astype(BF16)

        mid = GLA_SUB // 2 - 1
        refs = [cum[s * GLA_SUB + mid:s * GLA_SUB + mid + 1, :] for s in range(N_SUB)]
        last = cum[CHUNK - 1:CHUNK, :]
        d = cum - jnp.concatenate([jnp.broadcast_to(r, (GLA_SUB, V_W)) for r in refs], axis=0)
        gq_e = gq * jnp.exp2(d)
        gk_e = gk * jnp.exp2(-d)
        zero_blk = jnp.zeros((GLA_SUB, V_W), BF16)
        q_sets, k_sets = [], []
        for t2 in range(N_SUB // 2):
            q_rows, k_rows = [], []
            for rb in range(N_SUB):
                s_lo, s_hi = 2 * t2, 2 * t2 + 1
                if s_lo > rb:
                    q_rows.append(zero_blk)
                else:
                    f_lo = jnp.exp2(refs[rb] - refs[s_lo])
                    f_hi = jnp.exp2(refs[rb] - refs[s_hi]) if s_hi <= rb else jnp.zeros_like(f_lo)
                    q_rows.append((blk(gq_e, rb) * jnp.where(dup_half == 0, f_lo, f_hi)).astype(BF16))
                if rb // 2 == t2:
                    k_rows.append(jnp.where(dup_half == rb % 2, blk(gk_e, rb), 0.0).astype(BF16))
                else:
                    k_rows.append(zero_blk)
            q_sets.append(jnp.concatenate(q_rows, axis=0))
            k_sets.append(jnp.concatenate(k_rows, axis=0))

        e_last = jnp.exp2(last)
        out["g_v"] = gv
        out["g_q"] = [jnp.concatenate([qs[:, hd * LANES:(hd + 1) * LANES] for qs in q_sets], axis=-1)
                      for hd in range(HEADS)]
        out["g_k"] = [jnp.concatenate([ks[:, hd * LANES:(hd + 1) * LANES] for ks in k_sets], axis=-1)
                      for hd in range(HEADS)]
        out["g_qe"], out["g_kdt"], out["g_dec"] = [], [], []
        for p in range(HEADS // 2):
            gq_n = natural_pair(gq_e, p)
            gk_n = natural_pair(gk_e, p)
            out["g_qe"].append(jnp.concatenate(
                [blk(gq_n, rb) * natural_pair(jnp.exp2(refs[rb]), p) for rb in range(N_SUB)],
                axis=0).astype(BF16))
            kd = jnp.concatenate(
                [blk(gk_n, rb) * natural_pair(jnp.exp2(last - refs[rb]), p) for rb in range(N_SUB)],
                axis=0)
            out["g_kdt"].append(kd.T.astype(BF16))
            out["g_dec"].append(jnp.broadcast_to(natural_pair(e_last, p), (CHUNK, LANES)).T)
        return out

    def stage_b(c, a):
        rows = slice(c * CHUNK, (c + 1) * CHUNK)
        g_s = [jnp.where(causal, _dot_nt(a["g_q"][hd], a["g_k"][hd]), 0.0).astype(BF16)
               for hd in range(HEADS)]

        v, s_all = a["r_v"], a["r_s"]
        o_inter = _dot(a["r_qd"], rstate_bd_ref[...])
        for p in range(HEADS // 2):
            ha, hb = 2 * p, 2 * p + 1
            s_pair = jnp.concatenate([s_all[ha * CHUNK:(ha + 1) * CHUNK],
                                      s_all[hb * CHUNK:(hb + 1) * CHUNK]], axis=-1)
            o_pair = _dot(s_pair, _block_diag2(v[:, ha * DV:(ha + 1) * DV], v[:, hb * DV:(hb + 1) * DV]))
            for hd in (ha, hb):
                cols = slice(hd * DV, (hd + 1) * DV)
                o = o_pair[:, (hd - ha) * DV:(hd - ha + 1) * DV] + o_inter[:, cols]
                o = _rms(o, ret_g_ref[:, cols])
                gate = z_ref[rows, OFF_RG + hd * DV:OFF_RG + (hd + 1) * DV]
                y_ref[rows, cols] = (o * gate).astype(BF16)
                st = rstate_ref[hd] * cdec_ref[hd] + _dot(a["r_kd"][hd], v[:, cols])
                rstate_ref[hd] = st
                st_bf = st.astype(BF16)
                rstate_bd_ref[hd * hw:(hd + 1) * hw, cols] = st_bf[:hw]
                rstate_bd_ref[half + hd * hw:half + (hd + 1) * hw, cols] = st_bf[hw:]

        gv = a["g_v"]
        for p in range(HEADS // 2):
            ha, hb = 2 * p, 2 * p + 1
            st = gstate_ref[p]
            o_pair = (_dot(a["g_qe"][p], st.astype(BF16))
                      + _dot(jnp.concatenate([g_s[ha], g_s[hb]], axis=-1),
                             _block_diag2(gv[:, ha * DV:(ha + 1) * DV], gv[:, hb * DV:(hb + 1) * DV])))
            dec_col = a["g_dec"][p]
            st = st * jnp.concatenate([dec_col, dec_col], axis=-1)
            gstate_ref[p] = st
            kdt = a["g_kdt"][p]
            for hd in (ha, hb):
                j = hd - ha
                cols = slice(hd * DV, (hd + 1) * DV)
                upd = _dot(kdt[j * DK:(j + 1) * DK], gv[:, cols])
                gstate_ref[p, j * DK:(j + 1) * DK, j * DV:(j + 1) * DV] = (
                    st[j * DK:(j + 1) * DK, j * DV:(j + 1) * DV] + upd)
                o = _rms(o_pair[:, j * DV:(j + 1) * DV], gla_g_ref[:, cols])
                gate = z_ref[rows, OFF_GG + hd * DV:OFF_GG + (hd + 1) * DV]
                y_ref[rows, V_W + hd * DV:V_W + (hd + 1) * DV] = (o * gate).astype(BF16)

    n_chunks = tile // CHUNK
    assert sum(MIX_SUB_CHUNKS) == n_chunks and min(MIX_SUB_CHUNKS) >= 2
    bounds = [sum(MIX_SUB_CHUNKS[:s]) for s in range(len(MIX_SUB_CHUNKS) + 1)]
    n_sub = len(MIX_SUB_CHUNKS)
    sub_rows = [slice(bounds[s] * CHUNK, bounds[s + 1] * CHUNK) for s in range(n_sub)]
    spread = lambda k, n: range(k * N_GROUPS // n, (k + 1) * N_GROUPS // n)
    pre_norm(sub_rows[0])
    for g in range(N_GROUPS):
        in_projection(sub_rows[0], g)
    ready = {0: stage_a(0)}
    for s in range(n_sub):
        for k, c in enumerate(range(bounds[s], bounds[s + 1])):
            if c + 1 < n_chunks:
                ready[c + 1] = stage_a(c + 1)
            stage_b(c, ready.pop(c))
            if s + 1 < n_sub and k < MIX_SUB_CHUNKS[s] - 1:
                if k == 0:
                    pre_norm(sub_rows[s + 1])
                for g in spread(k, MIX_SUB_CHUNKS[s] - 1):
                    in_projection(sub_rows[s + 1], g)
            if s >= 1:
                for g in spread(k, MIX_SUB_CHUNKS[s]):
                    out_projection(sub_rows[s - 1], g)
                if k == MIX_SUB_CHUNKS[s] - 1:
                    post_norm(sub_rows[s - 1])
    for g in range(N_GROUPS):
        out_projection(sub_rows[-1], g)
    post_norm(sub_rows[-1])


def _mixer(x, layer, cos_t, sin_t, pre_g, w_in, w_a2, b_a2, ret_g, gla_g, w_out, post_g, tables):
    b, t, d = x.shape
    dmat, qdec, kdec, cdec = tables
    const2 = lambda bi, ti: (0, 0)
    lay2 = lambda bi, ti: (layer, 0, 0)
    tok = lambda bi, ti: (bi, ti, 0)
    return pl.pallas_call(
        _mixer_kernel,
        grid=(b, t // MIX_TILE),
        in_specs=[
            pl.BlockSpec((None, MIX_TILE, d), tok),
            pl.BlockSpec((None, MIX_TILE, QK_W // 2), tok),
            pl.BlockSpec((None, MIX_TILE, QK_W // 2), tok),
            pl.BlockSpec((None, 1, d), lay2),
            pl.BlockSpec((None, d, IN_W), lay2, pipeline_mode=pl.Buffered(1)),
            pl.BlockSpec((None, GATE_RANK_PAD, 2 * QK_W), lay2),
            pl.BlockSpec((None, 1, 2 * QK_W), lay2),
            pl.BlockSpec((None, 1, V_W), lay2),
            pl.BlockSpec((None, 1, V_W), lay2),
            pl.BlockSpec((None, 2 * V_W, d), lay2, pipeline_mode=pl.Buffered(1)),
            pl.BlockSpec((None, 1, d), lay2),
            pl.BlockSpec((HEADS * CHUNK, CHUNK), const2),
            pl.BlockSpec((CHUNK, QK_W), const2),
            pl.BlockSpec((CHUNK, QK_W), const2),
            pl.BlockSpec(memory_space=pltpu.SMEM),
        ],
        out_specs=pl.BlockSpec((None, MIX_TILE, d), tok),
        out_shape=jax.ShapeDtypeStruct(x.shape, F32),
        scratch_shapes=[
            pltpu.VMEM((MIX_TILE, d), BF16),
            pltpu.VMEM((MIX_TILE, IN_W), F32),
            pltpu.VMEM((MIX_TILE, 2 * QK_W), F32),
            pltpu.VMEM((MIX_TILE, 2 * V_W), BF16),
            pltpu.VMEM((HEADS, DK, DV), F32),
            pltpu.VMEM((QK_W, V_W), BF16),
            pltpu.VMEM((HEADS // 2, 2 * DK, 2 * DV), F32),
        ],
        compiler_params=pltpu.CompilerParams(
            dimension_semantics=("arbitrary", "arbitrary"), vmem_limit_bytes=VMEM_LIMIT),
        name="mixer",
    )(x, cos_t, sin_t, pre_g, w_in, w_a2, b_a2, ret_g, gla_g, w_out, post_g,
      dmat, qdec, kdec, cdec)


def _retention_tables():
    c = CHUNK
    log_gamma = jnp.log1p(-jnp.exp2(-5.0 - jnp.arange(HEADS, dtype=F32)))
    idx = jnp.arange(c, dtype=F32)
    rel = idx[:, None] - idx[None, :]
    causal = rel >= 0
    decay_in = jnp.where(causal[None],
                         jnp.exp(jnp.where(causal, rel, 0.0)[None] * log_gamma[:, None, None]), 0.0)
    q_dec = jnp.exp((idx + 1.0)[None, :] * log_gamma[:, None])
    k_dec = jnp.exp((c - 1.0 - idx)[None, :] * log_gamma[:, None])
    chunk_dec = jnp.exp(c * log_gamma)
    head_of_lane = (np.arange(QK_W) % (QK_W // 2)) // (DK // 2)
    dmat = decay_in.reshape(HEADS * c, c)
    qdec = q_dec.T[:, head_of_lane]
    kdec = k_dec.T[:, head_of_lane]
    return dmat, qdec, kdec, chunk_dec


def _rotary_cols(w):
    lead = w.shape[:-1]
    w = w.reshape(lead + (HEADS, 2, DK // 2))
    return jnp.swapaxes(w, -3, -2).reshape(lead + (QK_W,))


def _dup_cols(w):
    lead = w.shape[:-1]
    w = w.reshape(lead + (HEADS, 1, DK))
    return jnp.broadcast_to(w, lead + (HEADS, 2, DK)).reshape(lead + (2 * QK_W,))


def _relayout_w_in(w_in):
    sizes = (QK_W, QK_W, V_W, V_W, QK_W, QK_W, V_W, GATE_RANK, V_W)
    pts = np.cumsum(sizes)[:-1]
    rq, rk, rv, rg, gq, gk, gv, ga, gg = jnp.split(w_in, pts, axis=-1)
    ga = jnp.pad(ga, ((0, 0), (0, 0), (0, GATE_RANK_PAD - GATE_RANK)))
    return jnp.concatenate([ga, _dup_cols(gk), _dup_cols(gq) * QK_SCALE, gv,
                            _rotary_cols(rk) * QK_SCALE, _rotary_cols(rq), rv, gg, rg], axis=-1)


def _memkv_kernel(mem_ref, g_ref, w_ref, k_ref, v_ref):
    nb, m, d = mem_ref.shape
    x = mem_ref[...].reshape(nb * m, d)
    kv = _dot(_rms(x, g_ref[...]).astype(BF16), w_ref[...])
    for i in range(nb):
        k_ref[i] = kv[i * m:(i + 1) * m, :D_MODEL].T.astype(BF16)
        v_ref[i] = kv[i * m:(i + 1) * m, D_MODEL:].astype(BF16)


def _memkv(mem, layer, g, w_kv):
    b, m, d = mem.shape
    nb = MEMKV_BATCH
    lay2 = lambda i: (layer, 0, 0)
    return pl.pallas_call(
        _memkv_kernel,
        grid=(b // nb,),
        in_specs=[
            pl.BlockSpec((nb, m, d), lambda i: (i, 0, 0)),
            pl.BlockSpec((None, 1, d), lay2),
            pl.BlockSpec((None, d, 2 * d), lay2),
        ],
        out_specs=[pl.BlockSpec((nb, d, m), lambda i: (i, 0, 0)),
                   pl.BlockSpec((nb, m, d), lambda i: (i, 0, 0))],
        out_shape=[jax.ShapeDtypeStruct((b, d, m), BF16), jax.ShapeDtypeStruct((b, m, d), BF16)],
        compiler_params=pltpu.CompilerParams(
            dimension_semantics=("arbitrary",), vmem_limit_bytes=VMEM_LIMIT),
        name="memkv",
    )(mem, g, w_kv)


def _xattn_kernel(x_ref, kt_ref, v_ref, pre_g_ref, wq_ref, wo_ref, post_g_ref, o_ref, att_ref):
    for sub in range(XA_TILE // XA_SUB):
        rows = slice(sub * XA_SUB, (sub + 1) * XA_SUB)
        x = x_ref[rows, :]
        h = _rms(x, pre_g_ref[...]).astype(BF16)
        q = _dot(h, wq_ref[...])
        for hd in range(XA_HEADS):
            cols = slice(hd * XA_DH, (hd + 1) * XA_DH)
            s = _dot(q[:, cols].astype(BF16), kt_ref[cols, :])
            p = jnp.exp(s - jnp.max(s, axis=-1, keepdims=True))
            o = _dot(p.astype(BF16), v_ref[:, cols]) / jnp.sum(p, axis=-1, keepdims=True)
            att_ref[rows, cols] = o.astype(BF16)
        out = _dot(att_ref[rows, :], wo_ref[...])
        o_ref[rows, :] = x + _rms(out, post_g_ref[...])


def _xattn(x, layer, kt, v, pre_g, wq, wo, post_g):
    b, t, d = x.shape
    m = v.shape[1]
    lay2 = lambda bi, ti: (layer, 0, 0)
    tok = lambda bi, ti: (bi, ti, 0)
    mem = lambda bi, ti: (bi, 0, 0)
    return pl.pallas_call(
        _xattn_kernel,
        grid=(b, t // XA_TILE),
        in_specs=[
            pl.BlockSpec((None, XA_TILE, d), tok),
            pl.BlockSpec((None, d, m), mem),
            pl.BlockSpec((None, m, d), mem),
            pl.BlockSpec((None, 1, d), lay2),
            pl.BlockSpec((None, d, d), lay2),
            pl.BlockSpec((None, d, d), lay2),
            pl.BlockSpec((None, 1, d), lay2),
        ],
        out_specs=pl.BlockSpec((None, XA_TILE, d), tok),
        out_shape=jax.ShapeDtypeStruct(x.shape, F32),
        scratch_shapes=[pltpu.VMEM((XA_TILE, d), BF16)],
        compiler_params=pltpu.CompilerParams(
            dimension_semantics=("arbitrary", "arbitrary"), vmem_limit_bytes=VMEM_LIMIT),
        name="xattn",
    )(x, kt, v, pre_g, wq, wo, post_g)


def kernel(x, mem, positions, ffn1_pre_g, ffn1_w_gate, ffn1_w_up, ffn1_w_down, ffn1_post_g, mix_pre_g, w_in, w_a2, b_a2, ret_norm_g, gla_norm_g, w_out, mix_post_g, xa_pre_g, xa_mem_g, xa_w_q, xa_w_kv, xa_w_o, xa_post_g, ffn2_pre_g, ffn2_w_gate, ffn2_w_up, ffn2_w_down, ffn2_post_g):
    b, t, d = x.shape
    depth = w_in.shape[0]
    assert d == D_MODEL and t % MIX_TILE == 0 and t % XA_TILE == 0 and (b * t) % FFN_TILE == 0
    assert (b * t) % ROPE_TILE == 0 and b % MEMKV_BATCH == 0

    cos_t, sin_t = _rope_tables(positions)
    cos_t = cos_t.reshape(b, t, QK_W // 2)
    sin_t = sin_t.reshape(b, t, QK_W // 2)
    tables = _retention_tables()

    row = lambda g: g.reshape(depth, 1, -1)
    bf = lambda w: w.astype(BF16)
    ffn1 = (row(ffn1_pre_g), bf(ffn1_w_gate), bf(ffn1_w_up), bf(ffn1_w_down), row(ffn1_post_g))
    ffn2 = (row(ffn2_pre_g), bf(ffn2_w_gate), bf(ffn2_w_up), bf(ffn2_w_down), row(ffn2_post_g))
    w_a2_p = bf(jnp.pad(_dup_cols(w_a2), ((0, 0), (0, GATE_RANK_PAD - GATE_RANK), (0, 0))))
    mix = (row(mix_pre_g), _relayout_w_in(bf(w_in)), w_a2_p, row(_dup_cols(b_a2)),
           row(ret_norm_g), row(gla_norm_g), bf(w_out), row(mix_post_g))
    xa_kv = (row(xa_mem_g), bf(xa_w_kv))
    xa = (row(xa_pre_g), bf(xa_w_q) * XA_SCALE, bf(xa_w_o), row(xa_post_g))

    for l in range(depth):
        x = _ffn(x.reshape(b * t, d), l, *ffn1).reshape(b, t, d)
        x = _mixer(x, l, cos_t, sin_t, *mix, tables)
        k_mem, v_mem = _memkv(mem, l, *xa_kv)
        x = _xattn(x, l, k_mem, v_mem, *xa)
        x = _ffn(x.reshape(b * t, d), l, *ffn2).reshape(b, t, d)
    return x
```

```python
import numpy as np
import jax
import jax.numpy as jnp
from jax import lax
from jax.experimental import pallas as pl
from jax.experimental.pallas import tpu as pltpu

D_MODEL = 1024
D_FF = 2816
EPS = 1e-6
ROPE_BASE = 10000.0

LANES = 128
HEADS = 4
DK = 64
DV = 128
QK_W = HEADS * DK
V_W = HEADS * DV
CHUNK = 128
GLA_SUB = 32
N_SUB = CHUNK // GLA_SUB
GATE_RANK = 16
GATE_RANK_PAD = 128
GATE_TAU = 16.0
QK_SCALE = DK ** -0.5

XA_HEADS = 4
XA_DH = D_MODEL // XA_HEADS
XA_SCALE = XA_DH ** -0.5
assert XA_SCALE == 2.0 ** -4 and QK_SCALE == 2.0 ** -3

OFF_GA = 0
OFF_GK = OFF_GA + GATE_RANK_PAD
OFF_GQ = OFF_GK + 2 * QK_W
OFF_GV = OFF_GQ + 2 * QK_W
OFF_RK = OFF_GV + V_W
OFF_RQ = OFF_RK + QK_W
OFF_RV = OFF_RQ + QK_W
OFF_GG = OFF_RV + V_W
OFF_RG = OFF_GG + V_W
IN_W = OFF_RG + V_W
N_GROUPS = 4
IN_GROUPS = (0, OFF_GV, OFF_RV, OFF_RG, IN_W)
OUT_GROUP = D_MODEL // N_GROUPS
LOG2E = 1.4426950408889634

FFN_TILE = 1024
FFN_SUB = 512
FFN_CHUNK = 256
MIX_TILE = 1024
MIX_SUB = 512
XA_TILE = 1024
XA_SUB = 512
ROPE_TILE = 1024
MEMKV_BATCH = 4
VMEM_LIMIT = 56 * 1024 * 1024

F32 = jnp.float32
BF16 = jnp.bfloat16


def _rms(x, g):
    return x * lax.rsqrt(jnp.mean(x * x, axis=-1, keepdims=True) + EPS) * g


def _silu(x):
    return x * (1.0 / (1.0 + jnp.exp(-x)))


def _dot(a, b):
    return jnp.dot(a, b, preferred_element_type=F32)


def _dot_nt(a, b):
    return lax.dot_general(a, b, (((1,), (1,)), ((), ())), preferred_element_type=F32)


def _block_diag2(a, b):
    za = jnp.zeros(a.shape, a.dtype)
    return jnp.concatenate(
        [jnp.concatenate([a, za], axis=-1), jnp.concatenate([za, b], axis=-1)], axis=0)


def _ffn_rows(x, rows, pre_g_ref, wg_ref, wu_ref, wd_ref, post_g_ref, u_ref):
    h = _rms(x, pre_g_ref[...]).astype(BF16)
    for lo in range(0, D_FF, FFN_CHUNK):
        a = _dot(h, wg_ref[:, lo:lo + FFN_CHUNK])
        b = _dot(h, wu_ref[:, lo:lo + FFN_CHUNK])
        u_ref[rows, lo:lo + FFN_CHUNK] = (_silu(a) * b).astype(BF16)
    acc = _dot(u_ref[rows, :], wd_ref[...])
    return x + _rms(acc, 0.5 * post_g_ref[...])


def _ffn_kernel(x_ref, pre_g_ref, wg_ref, wu_ref, wd_ref, post_g_ref, o_ref, u_ref):
    for s in range(FFN_TILE // FFN_SUB):
        rows = slice(s * FFN_SUB, (s + 1) * FFN_SUB)
        o_ref[rows, :] = _ffn_rows(x_ref[rows, :], rows, pre_g_ref, wg_ref, wu_ref, wd_ref,
                                   post_g_ref, u_ref)


def _ffn(x2d, layer, pre_g, wg, wu, wd, post_g):
    n = x2d.shape[0]
    lay2 = lambda i: (layer, 0, 0)
    return pl.pallas_call(
        _ffn_kernel,
        grid=(n // FFN_TILE,),
        in_specs=[
            pl.BlockSpec((FFN_TILE, D_MODEL), lambda i: (i, 0)),
            pl.BlockSpec((None, 1, D_MODEL), lay2),
            pl.BlockSpec((None, D_MODEL, D_FF), lay2, pipeline_mode=pl.Buffered(1)),
            pl.BlockSpec((None, D_MODEL, D_FF), lay2, pipeline_mode=pl.Buffered(1)),
            pl.BlockSpec((None, D_FF, D_MODEL), lay2, pipeline_mode=pl.Buffered(1)),
            pl.BlockSpec((None, 1, D_MODEL), lay2),
        ],
        out_specs=pl.BlockSpec((FFN_TILE, D_MODEL), lambda i: (i, 0)),
        out_shape=jax.ShapeDtypeStruct(x2d.shape, F32),
        scratch_shapes=[pltpu.VMEM((FFN_TILE, D_FF), BF16)],
        compiler_params=pltpu.CompilerParams(
            dimension_semantics=("arbitrary",), vmem_limit_bytes=VMEM_LIMIT),
        name="ffn",
    )(x2d, pre_g, wg, wu, wd, post_g)


def _rope_kernel(pos_ref, freq_ref, cos_ref, sin_ref):
    ang = freq_ref[...] * pos_ref[...]
    cos_ref[...] = jnp.concatenate([jnp.cos(ang)] * HEADS, axis=0).T
    sin_ref[...] = jnp.concatenate([jnp.sin(ang)] * HEADS, axis=0).T


def _rope_tables(positions):
    b, t = positions.shape
    n = b * t
    pos = positions.astype(F32).reshape(n // ROPE_TILE, 1, ROPE_TILE)
    inv_freq = ROPE_BASE ** (-jnp.arange(0, DK, 2, dtype=F32) / DK)
    freq = inv_freq.reshape(DK // 2, 1)
    out = jax.ShapeDtypeStruct((n, QK_W // 2), F32)
    return pl.pallas_call(
        _rope_kernel,
        grid=(n // ROPE_TILE,),
        in_specs=[
            pl.BlockSpec((None, 1, ROPE_TILE), lambda i: (i, 0, 0)),
            pl.BlockSpec((DK // 2, 1), lambda i: (0, 0)),
        ],
        out_specs=[pl.BlockSpec((ROPE_TILE, QK_W // 2), lambda i: (i, 0))] * 2,
        out_shape=[out, out],
        compiler_params=pltpu.CompilerParams(dimension_semantics=("arbitrary",)),
        name="rope_tables",
    )(pos, freq)


def _mixer_kernel(x_ref, cos_ref, sin_ref, pre_g_ref, w_in_ref, w_a2_ref, b_a2_ref,
                  ret_g_ref, gla_g_ref, w_out_ref, post_g_ref,
                  dmat_ref, qdec_ref, kdec_ref, cdec_ref,
                  o_ref,
                  h_ref, z_ref, la_ref, y_ref, rstate_ref, rstate_bd_ref, gstate_ref):
    tile = x_ref.shape[0]

    @pl.when(pl.program_id(1) == 0)
    def _():
        rstate_ref[...] = jnp.zeros(rstate_ref.shape, F32)
        rstate_bd_ref[...] = jnp.zeros(rstate_bd_ref.shape, BF16)
        gstate_ref[...] = jnp.zeros(gstate_ref.shape, F32)

    def pre_norm(srows):
        h_ref[srows, :] = _rms(x_ref[srows, :], pre_g_ref[...]).astype(BF16)

    def in_projection(srows, g):
        lo, hi = IN_GROUPS[g], IN_GROUPS[g + 1]
        z_ref[srows, lo:hi] = _dot(h_ref[srows, :], w_in_ref[:, lo:hi])
        if g == 0:
            ga = z_ref[srows, OFF_GA:OFF_GA + GATE_RANK_PAD].astype(BF16)
            pre = _dot(ga, w_a2_ref[...]) + b_a2_ref[...]
            la_ref[srows, :] = ((jnp.minimum(pre, 0.0) - jnp.log1p(jnp.exp(-jnp.abs(pre))))
                                * (LOG2E / GATE_TAU))
        if g == N_GROUPS - 1:
            z_ref[srows, OFF_GG:IN_W] = _silu(z_ref[srows, OFF_GG:IN_W])

    def out_projection(srows, g):
        cols = slice(g * OUT_GROUP, (g + 1) * OUT_GROUP)
        o_ref[srows, cols] = _dot(y_ref[srows, :], w_out_ref[:, cols])

    def post_norm(srows):
        o_ref[srows, :] = x_ref[srows, :] + _rms(o_ref[srows, :], post_g_ref[...])

    half = QK_W // 2
    hw = DK // 2
    lane_qk = lax.broadcasted_iota(jnp.int32, (1, QK_W), 1)
    ret_masks = [(lane_qk % half) // hw == hd for hd in range(HEADS)]

    ri = lax.broadcasted_iota(jnp.int32, (CHUNK, CHUNK), 0)
    ci = lax.broadcasted_iota(jnp.int32, (CHUNK, CHUNK), 1)
    causal = ri >= ci
    tri_bf = jnp.where(causal, 1.0, 0.0).astype(BF16)
    tri2_bf = jnp.concatenate([tri_bf, tri_bf], axis=1)
    lane_v = lax.broadcasted_iota(jnp.int32, (1, V_W), 1)
    dup_half = (lane_v % LANES) // DK
    first_half = lax.broadcasted_iota(jnp.int32, (1, LANES), 1) < DK

    def natural_pair(a, p):
        return jnp.where(first_half, a[:, 2 * p * LANES:(2 * p + 1) * LANES],
                         a[:, (2 * p + 1) * LANES:(2 * p + 2) * LANES])

    blk = lambda a, rb: a[rb * GLA_SUB:(rb + 1) * GLA_SUB]

    def stage_a(c):
        rows = slice(c * CHUNK, (c + 1) * CHUNK)
        out = {}

        cosv = cos_ref[rows, :]
        sinv = sin_ref[rows, :]

        def rope(off):
            x1 = z_ref[rows, off:off + half]
            x2 = z_ref[rows, off + half:off + QK_W]
            return jnp.concatenate([x1 * cosv - x2 * sinv, x1 * sinv + x2 * cosv], axis=-1)

        q = rope(OFF_RQ)
        k = rope(OFF_RK)
        v = z_ref[rows, OFF_RV:OFF_RV + V_W].astype(BF16)
        qstack = jnp.concatenate([jnp.where(m, q, 0.0) for m in ret_masks], axis=0).astype(BF16)
        s_all = (_dot_nt(qstack, k.astype(BF16)) * dmat_ref[...]).astype(BF16)
        kdt = (k * kdec_ref[...]).T
        out["r_v"] = v
        out["r_s"] = s_all
        out["r_qd"] = (q * qdec_ref[...]).astype(BF16)
        out["r_kd"] = [jnp.concatenate([kdt[hd * hw:(hd + 1) * hw],
                                        kdt[half + hd * hw:half + (hd + 1) * hw]], axis=0).astype(BF16)
                       for hd in range(HEADS)]

        la = la_ref[rows, :]
        a1 = la.astype(BF16)
        a2 = (la - a1.astype(F32)).astype(BF16)
        cum = _dot(tri2_bf, jnp.concatenate([a1, a2], axis=0))
        gq = z_ref[rows, OFF_GQ:OFF_GQ + 2 * QK_W]
        gk = z_ref[rows, OFF_GK:OFF_GK + 2 * QK_W]
        gv = z_ref[rows, OFF_GV:OFF_GV + V_W].astype(BF16)

        mid = GLA_SUB // 2 - 1
        refs = [cum[s * GLA_SUB + mid:s * GLA_SUB + mid + 1, :] for s in range(N_SUB)]
        last = cum[CHUNK - 1:CHUNK, :]
        d = cum - jnp.concatenate([jnp.broadcast_to(r, (GLA_SUB, V_W)) for r in refs], axis=0)
        gq_e = gq * jnp.exp2(d)
        gk_e = gk * jnp.exp2(-d)
        zero_blk = jnp.zeros((GLA_SUB, V_W), BF16)
        q_sets, k_sets = [], []
        for t2 in range(N_SUB // 2):
            q_rows, k_rows = [], []
            for rb in range(N_SUB):
                s_lo, s_hi = 2 * t2, 2 * t2 + 1
                if s_lo > rb:
                    q_rows.append(zero_blk)
                else:
                    f_lo = jnp.exp2(refs[rb] - refs[s_lo])
                    f_hi = jnp.exp2(refs[rb] - refs[s_hi]) if s_hi <= rb else jnp.zeros_like(f_lo)
                    q_rows.append((blk(gq_e, rb) * jnp.where(dup_half == 0, f_lo, f_hi)).astype(BF16))
                if rb // 2 == t2:
                    k_rows.append(jnp.where(dup_half == rb % 2, blk(gk_e, rb), 0.0).astype(BF16))
                else:
                    k_rows.append(zero_blk)
            q_sets.append(jnp.concatenate(q_rows, axis=0))
            k_sets.append(jnp.concatenate(k_rows, axis=0))

        e_last = jnp.exp2(last)
        out["g_v"] = gv
        out["g_q"] = [jnp.concatenate([qs[:, hd * LANES:(hd + 1) * LANES] for qs in q_sets], axis=-1)
                      for hd in range(HEADS)]
        out["g_k"] = [jnp.concatenate([ks[:, hd * LANES:(hd + 1) * LANES] for ks in k_sets], axis=-1)
                      for hd in range(HEADS)]
        out["g_qe"], out["g_kdt"], out["g_dec"] = [], [], []
        for p in range(HEADS // 2):
            gq_n = natural_pair(gq_e, p)
            gk_n = natural_pair(gk_e, p)
            out["g_qe"].append(jnp.concatenate(
                [blk(gq_n, rb) * natural_pair(jnp.exp2(refs[rb]), p) for rb in range(N_SUB)],
                axis=0).astype(BF16))
            kd = jnp.concatenate(
                [blk(gk_n, rb) * natural_pair(jnp.exp2(last - refs[rb]), p) for rb in range(N_SUB)],
                axis=0)
            out["g_kdt"].append(kd.T.astype(BF16))
            out["g_dec"].append(jnp.broadcast_to(natural_pair(e_last, p), (CHUNK, LANES)).T)
        return out

    def stage_b(c, a):
        rows = slice(c * CHUNK, (c + 1) * CHUNK)
        g_s = [jnp.where(causal, _dot_nt(a["g_q"][hd], a["g_k"][hd]), 0.0).astype(BF16)
               for hd in range(HEADS)]

        v, s_all = a["r_v"], a["r_s"]
        o_inter = _dot(a["r_qd"], rstate_bd_ref[...])
        for p in range(HEADS // 2):
            ha, hb = 2 * p, 2 * p + 1
            s_pair = jnp.concatenate([s_all[ha * CHUNK:(ha + 1) * CHUNK],
                                      s_all[hb * CHUNK:(hb + 1) * CHUNK]], axis=-1)
            o_pair = _dot(s_pair, _block_diag2(v[:, ha * DV:(ha + 1) * DV], v[:, hb * DV:(hb + 1) * DV]))
            for hd in (ha, hb):
                cols = slice(hd * DV, (hd + 1) * DV)
                o = o_pair[:, (hd - ha) * DV:(hd - ha + 1) * DV] + o_inter[:, cols]
                o = _rms(o, ret_g_ref[:, cols])
                gate = z_ref[rows, OFF_RG + hd * DV:OFF_RG + (hd + 1) * DV]
                y_ref[rows, cols] = (o * gate).astype(BF16)
                st = rstate_ref[hd] * cdec_ref[hd] + _dot(a["r_kd"][hd], v[:, cols])
                rstate_ref[hd] = st
                st_bf = st.astype(BF16)
                rstate_bd_ref[hd * hw:(hd + 1) * hw, cols] = st_bf[:hw]
                rstate_bd_ref[half + hd * hw:half + (hd + 1) * hw, cols] = st_bf[hw:]

        gv = a["g_v"]
        for p in range(HEADS // 2):
            ha, hb = 2 * p, 2 * p + 1
            st = gstate_ref[p]
            o_pair = (_dot(a["g_qe"][p], st.astype(BF16))
                      + _dot(jnp.concatenate([g_s[ha], g_s[hb]], axis=-1),
                             _block_diag2(gv[:, ha * DV:(ha + 1) * DV], gv[:, hb * DV:(hb + 1) * DV])))
            dec_col = a["g_dec"][p]
            st = st * jnp.concatenate([dec_col, dec_col], axis=-1)
            gstate_ref[p] = st
            kdt = a["g_kdt"][p]
            for hd in (ha, hb):
                j = hd - ha
                cols = slice(hd * DV, (hd + 1) * DV)
                upd = _dot(kdt[j * DK:(j + 1) * DK], gv[:, cols])
                gstate_ref[p, j * DK:(j + 1) * DK, j * DV:(j + 1) * DV] = (
                    st[j * DK:(j + 1) * DK, j * DV:(j + 1) * DV] + upd)
                o = _rms(o_pair[:, j * DV:(j + 1) * DV], gla_g_ref[:, cols])
                gate = z_ref[rows, OFF_GG + hd * DV:OFF_GG + (hd + 1) * DV]
                y_ref[rows, V_W + hd * DV:V_W + (hd + 1) * DV] = (o * gate).astype(BF16)

    n_sub_chunks = MIX_SUB // CHUNK
    assert tile == 2 * MIX_SUB and n_sub_chunks == N_GROUPS
    rows_a, rows_b = slice(0, MIX_SUB), slice(MIX_SUB, tile)
    pre_norm(rows_a)
    for g in range(N_GROUPS):
        in_projection(rows_a, g)
    pre_norm(rows_b)
    n_chunks = 2 * n_sub_chunks
    fillers = {0: [(in_projection, rows_b, 0), (in_projection, rows_b, 1)],
               1: [(in_projection, rows_b, 2), (in_projection, rows_b, 3)],
               4: [(out_projection, rows_a, 0)],
               5: [(out_projection, rows_a, 1)],
               6: [(out_projection, rows_a, 2)],
               7: [(out_projection, rows_a, 3)]}
    order = {0: [1], 1: [2], 2: [3, 4, 5], 3: [6, 7]}
    ready = {0: stage_a(0)}
    for c in range(n_chunks):
        for nxt in order.get(c, []):
            ready[nxt] = stage_a(nxt)
        stage_b(c, ready.pop(c))
        for fn, srows, g in fillers.get(c, []):
            fn(srows, g)
    post_norm(rows_a)
    for g in range(N_GROUPS):
        out_projection(rows_b, g)
    post_norm(rows_b)


def _mixer(x, layer, cos_t, sin_t, pre_g, w_in, w_a2, b_a2, ret_g, gla_g, w_out, post_g, tables):
    b, t, d = x.shape
    dmat, qdec, kdec, cdec = tables
    const2 = lambda bi, ti: (0, 0)
    lay2 = lambda bi, ti: (layer, 0, 0)
    tok = lambda bi, ti: (bi, ti, 0)
    return pl.pallas_call(
        _mixer_kernel,
        grid=(b, t // MIX_TILE),
        in_specs=[
            pl.BlockSpec((None, MIX_TILE, d), tok),
            pl.BlockSpec((None, MIX_TILE, QK_W // 2), tok),
            pl.BlockSpec((None, MIX_TILE, QK_W // 2), tok),
            pl.BlockSpec((None, 1, d), lay2),
            pl.BlockSpec((None, d, IN_W), lay2, pipeline_mode=pl.Buffered(1)),
            pl.BlockSpec((None, GATE_RANK_PAD, 2 * QK_W), lay2),
            pl.BlockSpec((None, 1, 2 * QK_W), lay2),
            pl.BlockSpec((None, 1, V_W), lay2),
            pl.BlockSpec((None, 1, V_W), lay2),
            pl.BlockSpec((None, 2 * V_W, d), lay2, pipeline_mode=pl.Buffered(1)),
            pl.BlockSpec((None, 1, d), lay2),
            pl.BlockSpec((HEADS * CHUNK, CHUNK), const2),
            pl.BlockSpec((CHUNK, QK_W), const2),
            pl.BlockSpec((CHUNK, QK_W), const2),
            pl.BlockSpec(memory_space=pltpu.SMEM),
        ],
        out_specs=pl.BlockSpec((None, MIX_TILE, d), tok),
        out_shape=jax.ShapeDtypeStruct(x.shape, F32),
        scratch_shapes=[
            pltpu.VMEM((MIX_TILE, d), BF16),
            pltpu.VMEM((MIX_TILE, IN_W), F32),
            pltpu.VMEM((MIX_TILE, 2 * QK_W), F32),
            pltpu.VMEM((MIX_TILE, 2 * V_W), BF16),
            pltpu.VMEM((HEADS, DK, DV), F32),
            pltpu.VMEM((QK_W, V_W), BF16),
            pltpu.VMEM((HEADS // 2, 2 * DK, 2 * DV), F32),
        ],
        compiler_params=pltpu.CompilerParams(
            dimension_semantics=("arbitrary", "arbitrary"), vmem_limit_bytes=VMEM_LIMIT),
        name="mixer",
    )(x, cos_t, sin_t, pre_g, w_in, w_a2, b_a2, ret_g, gla_g, w_out, post_g,
      dmat, qdec, kdec, cdec)


def _retention_tables():
    c = CHUNK
    log_gamma = jnp.log1p(-jnp.exp2(-5.0 - jnp.arange(HEADS, dtype=F32)))
    idx = jnp.arange(c, dtype=F32)
    rel = idx[:, None] - idx[None, :]
    causal = rel >= 0
    decay_in = jnp.where(causal[None],
                         jnp.exp(jnp.where(causal, rel, 0.0)[None] * log_gamma[:, None, None]), 0.0)
    q_dec = jnp.exp((idx + 1.0)[None, :] * log_gamma[:, None])
    k_dec = jnp.exp((c - 1.0 - idx)[None, :] * log_gamma[:, None])
    chunk_dec = jnp.exp(c * log_gamma)
    head_of_lane = (np.arange(QK_W) % (QK_W // 2)) // (DK // 2)
    dmat = decay_in.reshape(HEADS * c, c)
    qdec = q_dec.T[:, head_of_lane]
    kdec = k_dec.T[:, head_of_lane]
    return dmat, qdec, kdec, chunk_dec


def _rotary_cols(w):
    lead = w.shape[:-1]
    w = w.reshape(lead + (HEADS, 2, DK // 2))
    return jnp.swapaxes(w, -3, -2).reshape(lead + (QK_W,))


def _dup_cols(w):
    lead = w.shape[:-1]
    w = w.reshape(lead + (HEADS, 1, DK))
    return jnp.broadcast_to(w, lead + (HEADS, 2, DK)).reshape(lead + (2 * QK_W,))


def _relayout_w_in(w_in):
    sizes = (QK_W, QK_W, V_W, V_W, QK_W, QK_W, V_W, GATE_RANK, V_W)
    pts = np.cumsum(sizes)[:-1]
    rq, rk, rv, rg, gq, gk, gv, ga, gg = jnp.split(w_in, pts, axis=-1)
    ga = jnp.pad(ga, ((0, 0), (0, 0), (0, GATE_RANK_PAD - GATE_RANK)))
    return jnp.concatenate([ga, _dup_cols(gk), _dup_cols(gq) * QK_SCALE, gv,
                            _rotary_cols(rk) * QK_SCALE, _rotary_cols(rq), rv, gg, rg], axis=-1)


def _memkv_kernel(mem_ref, g_ref, w_ref, k_ref, v_ref):
    nb, m, d = mem_ref.shape
    x = mem_ref[...].reshape(nb * m, d)
    kv = _dot(_rms(x, g_ref[...]).astype(BF16), w_ref[...])
    for i in range(nb):
        k_ref[i] = kv[i * m:(i + 1) * m, :D_MODEL].T.astype(BF16)
        v_ref[i] = kv[i * m:(i + 1) * m, D_MODEL:].astype(BF16)


def _memkv(mem, layer, g, w_kv):
    b, m, d = mem.shape
    nb = MEMKV_BATCH
    lay2 = lambda i: (layer, 0, 0)
    return pl.pallas_call(
        _memkv_kernel,
        grid=(b // nb,),
        in_specs=[
            pl.BlockSpec((nb, m, d), lambda i: (i, 0, 0)),
            pl.BlockSpec((None, 1, d), lay2),
            pl.BlockSpec((None, d, 2 * d), lay2),
        ],
        out_specs=[pl.BlockSpec((nb, d, m), lambda i: (i, 0, 0)),
                   pl.BlockSpec((nb, m, d), lambda i: (i, 0, 0))],
        out_shape=[jax.ShapeDtypeStruct((b, d, m), BF16), jax.ShapeDtypeStruct((b, m, d), BF16)],
        compiler_params=pltpu.CompilerParams(
            dimension_semantics=("arbitrary",), vmem_limit_bytes=VMEM_LIMIT),
        name="memkv",
    )(mem, g, w_kv)


def _xattn_rows(x, rows, kt_ref, v_ref, pre_g_ref, wq_ref, wo_ref, post_g_ref, att_ref):
    h = _rms(x, pre_g_ref[...]).astype(BF16)
    q = _dot(h, wq_ref[...])
    for hd in range(XA_HEADS):
        cols = slice(hd * XA_DH, (hd + 1) * XA_DH)
        s = _dot(q[:, cols].astype(BF16), kt_ref[cols, :])
        p = jnp.exp(s - jnp.max(s, axis=-1, keepdims=True))
        o = _dot(p.astype(BF16), v_ref[:, cols]) / jnp.sum(p, axis=-1, keepdims=True)
        att_ref[rows, cols] = o.astype(BF16)
    out = _dot(att_ref[rows, :], wo_ref[...])
    return x + _rms(out, post_g_ref[...])


def _xattn_ffn_kernel(x_ref, kt_ref, v_ref, xa_pre_g_ref, wq_ref, wo_ref, xa_post_g_ref,
                      pre_g_ref, wg_ref, wu_ref, wd_ref, post_g_ref, o_ref, att_ref, u_ref):
    for sub in range(XA_TILE // XA_SUB):
        rows = slice(sub * XA_SUB, (sub + 1) * XA_SUB)
        x = _xattn_rows(x_ref[rows, :], rows, kt_ref, v_ref, xa_pre_g_ref, wq_ref, wo_ref,
                        xa_post_g_ref, att_ref)
        o_ref[rows, :] = _ffn_rows(x, rows, pre_g_ref, wg_ref, wu_ref, wd_ref, post_g_ref, u_ref)


def _xattn_ffn(x, layer, kt, v, xa_pre_g, wq, wo, xa_post_g, pre_g, wg, wu, wd, post_g):
    b, t, d = x.shape
    m = v.shape[1]
    lay2 = lambda bi, ti: (layer, 0, 0)
    tok = lambda bi, ti: (bi, ti, 0)
    mem = lambda bi, ti: (bi, 0, 0)
    once = pl.Buffered(1)
    return pl.pallas_call(
        _xattn_ffn_kernel,
        grid=(b, t // XA_TILE),
        in_specs=[
            pl.BlockSpec((None, XA_TILE, d), tok),
            pl.BlockSpec((None, d, m), mem),
            pl.BlockSpec((None, m, d), mem),
            pl.BlockSpec((None, 1, d), lay2),
            pl.BlockSpec((None, d, d), lay2, pipeline_mode=once),
            pl.BlockSpec((None, d, d), lay2, pipeline_mode=once),
            pl.BlockSpec((None, 1, d), lay2),
            pl.BlockSpec((None, 1, d), lay2),
            pl.BlockSpec((None, d, D_FF), lay2, pipeline_mode=once),
            pl.BlockSpec((None, d, D_FF), lay2, pipeline_mode=once),
            pl.BlockSpec((None, D_FF, d), lay2, pipeline_mode=once),
            pl.BlockSpec((None, 1, d), lay2),
        ],
        out_specs=pl.BlockSpec((None, XA_TILE, d), tok),
        out_shape=jax.ShapeDtypeStruct(x.shape, F32),
        scratch_shapes=[pltpu.VMEM((XA_TILE, d), BF16), pltpu.VMEM((XA_TILE, D_FF), BF16)],
        compiler_params=pltpu.CompilerParams(
            dimension_semantics=("arbitrary", "arbitrary"), vmem_limit_bytes=VMEM_LIMIT),
        name="xattn_ffn",
    )(x, kt, v, xa_pre_g, wq, wo, xa_post_g, pre_g, wg, wu, wd, post_g)


def kernel(x, mem, positions, ffn1_pre_g, ffn1_w_gate, ffn1_w_up, ffn1_w_down, ffn1_post_g, mix_pre_g, w_in, w_a2, b_a2, ret_norm_g, gla_norm_g, w_out, mix_post_g, xa_pre_g, xa_mem_g, xa_w_q, xa_w_kv, xa_w_o, xa_post_g, ffn2_pre_g, ffn2_w_gate, ffn2_w_up, ffn2_w_down, ffn2_post_g):
    b, t, d = x.shape
    depth = w_in.shape[0]
    assert d == D_MODEL and t % MIX_TILE == 0 and t % XA_TILE == 0 and (b * t) % FFN_TILE == 0
    assert (b * t) % ROPE_TILE == 0 and b % MEMKV_BATCH == 0

    cos_t, sin_t = _rope_tables(positions)
    cos_t = cos_t.reshape(b, t, QK_W // 2)
    sin_t = sin_t.reshape(b, t, QK_W // 2)
    tables = _retention_tables()

    row = lambda g: g.reshape(depth, 1, -1)
    bf = lambda w: w.astype(BF16)
    ffn1 = (row(ffn1_pre_g), bf(ffn1_w_gate), bf(ffn1_w_up), bf(ffn1_w_down), row(ffn1_post_g))
    ffn2 = (row(ffn2_pre_g), bf(ffn2_w_gate), bf(ffn2_w_up), bf(ffn2_w_down), row(ffn2_post_g))
    w_a2_p = bf(jnp.pad(_dup_cols(w_a2), ((0, 0), (0, GATE_RANK_PAD - GATE_RANK), (0, 0))))
    mix = (row(mix_pre_g), _relayout_w_in(bf(w_in)), w_a2_p, row(_dup_cols(b_a2)),
           row(ret_norm_g), row(gla_norm_g), bf(w_out), row(mix_post_g))
    xa_kv = (row(xa_mem_g), bf(xa_w_kv))
    xa = (row(xa_pre_g), bf(xa_w_q) * XA_SCALE, bf(xa_w_o), row(xa_post_g))

    for l in range(depth):
        x = _ffn(x.reshape(b * t, d), l, *ffn1).reshape(b, t, d)
        x = _mixer(x, l, cos_t, sin_t, *mix, tables)
        k_mem, v_mem = _memkv(mem, l, *xa_kv)
        x = _xattn_ffn(x, l, k_mem, v_mem, *xa, *ffn2)
    return x
```

```python
import numpy as np
import jax
import jax.numpy as jnp
from jax import lax
from jax.experimental import pallas as pl
from jax.experimental.pallas import tpu as pltpu

D_MODEL = 1024
D_FF = 2816
EPS = 1e-6
ROPE_BASE = 10000.0

LANES = 128
HEADS = 4
DK = 64
DV = 128
QK_W = HEADS * DK
V_W = HEADS * DV
CHUNK = 128
GLA_SUB = 32
N_SUB = CHUNK // GLA_SUB
GATE_RANK = 16
GATE_RANK_PAD = 128
GATE_TAU = 16.0
QK_SCALE = DK ** -0.5

XA_HEADS = 4
XA_DH = D_MODEL // XA_HEADS
XA_SCALE = XA_DH ** -0.5
assert XA_SCALE == 2.0 ** -4 and QK_SCALE == 2.0 ** -3

OFF_GA = 0
OFF_GK = OFF_GA + GATE_RANK_PAD
OFF_GQ = OFF_GK + 2 * QK_W
OFF_GV = OFF_GQ + 2 * QK_W
OFF_RK = OFF_GV + V_W
OFF_RQ = OFF_RK + QK_W
OFF_RV = OFF_RQ + QK_W
OFF_GG = OFF_RV + V_W
OFF_RG = OFF_GG + V_W
IN_W = OFF_RG + V_W
N_GROUPS = 4
IN_GROUPS = (0, OFF_GV, OFF_RV, OFF_RG, IN_W)
OUT_GROUP = D_MODEL // N_GROUPS
LOG2E = 1.4426950408889634

FFN_TILE = 1024
FFN_SUB = 512
FFN_CHUNK = 256
MIX_TILE = 1024
MIX_SUB = 512
XA_TILE = 1024
XA_SUB = 512
ROPE_TILE = 1024
MEMKV_BATCH = 4
VMEM_LIMIT = 56 * 1024 * 1024

F32 = jnp.float32
BF16 = jnp.bfloat16


def _rms(x, g):
    return x * lax.rsqrt(jnp.mean(x * x, axis=-1, keepdims=True) + EPS) * g


def _silu(x):
    return x * (1.0 / (1.0 + jnp.exp(-x)))


def _dot(a, b):
    return jnp.dot(a, b, preferred_element_type=F32)


def _dot_nt(a, b):
    return lax.dot_general(a, b, (((1,), (1,)), ((), ())), preferred_element_type=F32)


def _block_diag2(a, b):
    za = jnp.zeros(a.shape, a.dtype)
    return jnp.concatenate(
        [jnp.concatenate([a, za], axis=-1), jnp.concatenate([za, b], axis=-1)], axis=0)


def _ffn_rows(x, rows, pre_g_ref, wg_ref, wu_ref, wd_ref, post_g_ref, u_ref):
    h = _rms(x, pre_g_ref[...]).astype(BF16)
    for lo in range(0, D_FF, FFN_CHUNK):
        a = _dot(h, wg_ref[:, lo:lo + FFN_CHUNK])
        b = _dot(h, wu_ref[:, lo:lo + FFN_CHUNK])
        u_ref[rows, lo:lo + FFN_CHUNK] = (_silu(a) * b).astype(BF16)
    acc = _dot(u_ref[rows, :], wd_ref[...])
    return x + _rms(acc, 0.5 * post_g_ref[...])


def _ffn_kernel(x_ref, pre_g_ref, wg_ref, wu_ref, wd_ref, post_g_ref, o_ref, u_ref):
    for s in range(FFN_TILE // FFN_SUB):
        rows = slice(s * FFN_SUB, (s + 1) * FFN_SUB)
        o_ref[rows, :] = _ffn_rows(x_ref[rows, :], rows, pre_g_ref, wg_ref, wu_ref, wd_ref,
                                   post_g_ref, u_ref)


def _ffn(x2d, layer, pre_g, wg, wu, wd, post_g):
    n = x2d.shape[0]
    lay2 = lambda i: (layer, 0, 0)
    return pl.pallas_call(
        _ffn_kernel,
        grid=(n // FFN_TILE,),
        in_specs=[
            pl.BlockSpec((FFN_TILE, D_MODEL), lambda i: (i, 0)),
            pl.BlockSpec((None, 1, D_MODEL), lay2),
            pl.BlockSpec((None, D_MODEL, D_FF), lay2, pipeline_mode=pl.Buffered(1)),
            pl.BlockSpec((None, D_MODEL, D_FF), lay2, pipeline_mode=pl.Buffered(1)),
            pl.BlockSpec((None, D_FF, D_MODEL), lay2, pipeline_mode=pl.Buffered(1)),
            pl.BlockSpec((None, 1, D_MODEL), lay2),
        ],
        out_specs=pl.BlockSpec((FFN_TILE, D_MODEL), lambda i: (i, 0)),
        out_shape=jax.ShapeDtypeStruct(x2d.shape, F32),
        scratch_shapes=[pltpu.VMEM((FFN_TILE, D_FF), BF16)],
        compiler_params=pltpu.CompilerParams(
            dimension_semantics=("arbitrary",), vmem_limit_bytes=VMEM_LIMIT),
        name="ffn",
    )(x2d, pre_g, wg, wu, wd, post_g)


def _rope_kernel(pos_ref, freq_ref, cos_ref, sin_ref):
    ang = freq_ref[...] * pos_ref[...]
    cos_ref[...] = jnp.concatenate([jnp.cos(ang)] * HEADS, axis=0).T
    sin_ref[...] = jnp.concatenate([jnp.sin(ang)] * HEADS, axis=0).T


def _rope_tables(positions):
    b, t = positions.shape
    n = b * t
    pos = positions.astype(F32).reshape(n // ROPE_TILE, 1, ROPE_TILE)
    inv_freq = ROPE_BASE ** (-jnp.arange(0, DK, 2, dtype=F32) / DK)
    freq = inv_freq.reshape(DK // 2, 1)
    out = jax.ShapeDtypeStruct((n, QK_W // 2), F32)
    return pl.pallas_call(
        _rope_kernel,
        grid=(n // ROPE_TILE,),
        in_specs=[
            pl.BlockSpec((None, 1, ROPE_TILE), lambda i: (i, 0, 0)),
            pl.BlockSpec((DK // 2, 1), lambda i: (0, 0)),
        ],
        out_specs=[pl.BlockSpec((ROPE_TILE, QK_W // 2), lambda i: (i, 0))] * 2,
        out_shape=[out, out],
        compiler_params=pltpu.CompilerParams(dimension_semantics=("arbitrary",)),
        name="rope_tables",
    )(pos, freq)


def _mixer_kernel(x_ref, cos_ref, sin_ref, pre_g_ref, w_in_ref, w_a2_ref, b_a2_ref,
                  ret_g_ref, gla_g_ref, w_out_ref, post_g_ref,
                  dmat_ref, qdec_ref, kdec_ref, cdec_ref,
                  o_ref,
                  h_ref, z_ref, la_ref, y_ref, rstate_ref, rstate_bd_ref, gstate_ref):
    tile = x_ref.shape[0]

    @pl.when(pl.program_id(1) == 0)
    def _():
        rstate_ref[...] = jnp.zeros(rstate_ref.shape, F32)
        rstate_bd_ref[...] = jnp.zeros(rstate_bd_ref.shape, BF16)
        gstate_ref[...] = jnp.zeros(gstate_ref.shape, F32)

    def pre_norm(srows):
        h_ref[srows, :] = _rms(x_ref[srows, :], pre_g_ref[...]).astype(BF16)

    def in_projection(srows, g):
        lo, hi = IN_GROUPS[g], IN_GROUPS[g + 1]
        z_ref[srows, lo:hi] = _dot(h_ref[srows, :], w_in_ref[:, lo:hi])
        if g == 0:
            ga = z_ref[srows, OFF_GA:OFF_GA + GATE_RANK_PAD].astype(BF16)
            pre = _dot(ga, w_a2_ref[...]) + b_a2_ref[...]
            la_ref[srows, :] = ((jnp.minimum(pre, 0.0) - jnp.log1p(jnp.exp(-jnp.abs(pre))))
                                * (LOG2E / GATE_TAU))
        if g == N_GROUPS - 1:
            z_ref[srows, OFF_GG:IN_W] = _silu(z_ref[srows, OFF_GG:IN_W])

    def out_projection(srows, g):
        cols = slice(g * OUT_GROUP, (g + 1) * OUT_GROUP)
        o_ref[srows, cols] = _dot(y_ref[srows, :], w_out_ref[:, cols])

    def post_norm(srows):
        o_ref[srows, :] = x_ref[srows, :] + _rms(o_ref[srows, :], post_g_ref[...])

    half = QK_W // 2
    hw = DK // 2
    lane_qk = lax.broadcasted_iota(jnp.int32, (1, QK_W), 1)
    ret_masks = [(lane_qk % half) // hw == hd for hd in range(HEADS)]

    ri = lax.broadcasted_iota(jnp.int32, (CHUNK, CHUNK), 0)
    ci = lax.broadcasted_iota(jnp.int32, (CHUNK, CHUNK), 1)
    causal = ri >= ci
    tri_bf = jnp.where(causal, 1.0, 0.0).astype(BF16)
    tri2_bf = jnp.concatenate([tri_bf, tri_bf], axis=1)
    lane_v = lax.broadcasted_iota(jnp.int32, (1, V_W), 1)
    dup_half = (lane_v % LANES) // DK
    first_half = lax.broadcasted_iota(jnp.int32, (1, LANES), 1) < DK

    def natural_pair(a, p):
        return jnp.where(first_half, a[:, 2 * p * LANES:(2 * p + 1) * LANES],
                         a[:, (2 * p + 1) * LANES:(2 * p + 2) * LANES])

    blk = lambda a, rb: a[rb * GLA_SUB:(rb + 1) * GLA_SUB]

    def stage_a(c):
        rows = slice(c * CHUNK, (c + 1) * CHUNK)
        out = {}

        cosv = cos_ref[rows, :]
        sinv = sin_ref[rows, :]

        def rope(off):
            x1 = z_ref[rows, off:off + half]
            x2 = z_ref[rows, off + half:off + QK_W]
            return jnp.concatenate([x1 * cosv - x2 * sinv, x1 * sinv + x2 * cosv], axis=-1)

        q = rope(OFF_RQ)
        k = rope(OFF_RK)
        v = z_ref[rows, OFF_RV:OFF_RV + V_W].astype(BF16)
        qstack = jnp.concatenate([jnp.where(m, q, 0.0) for m in ret_masks], axis=0).astype(BF16)
        s_all = (_dot_nt(qstack, k.astype(BF16)) * dmat_ref[...]).astype(BF16)
        kdt = (k * kdec_ref[...]).T
        out["r_v"] = v
        out["r_s"] = s_all
        out["r_qd"] = (q * qdec_ref[...]).astype(BF16)
        out["r_kd"] = [jnp.concatenate([kdt[hd * hw:(hd + 1) * hw],
                                        kdt[half + hd * hw:half + (hd + 1) * hw]], axis=0).astype(BF16)
                       for hd in range(HEADS)]

        la = la_ref[rows, :]
        a1 = la.astype(BF16)
        a2 = (la - a1.astype(F32)).astype(BF16)
        cum = _dot(tri2_bf, jnp.concatenate([a1, a2], axis=0))
        gq = z_ref[rows, OFF_GQ:OFF_GQ + 2 * QK_W]
        gk = z_ref[rows, OFF_GK:OFF_GK + 2 * QK_W]
        gv = z_ref[rows, OFF_GV:OFF_GV + V_W].astype(BF16)

        mid = GLA_SUB // 2 - 1
        refs = [cum[s * GLA_SUB + mid:s * GLA_SUB + mid + 1, :] for s in range(N_SUB)]
        last = cum[CHUNK - 1:CHUNK, :]
        d = cum - jnp.concatenate([jnp.broadcast_to(r, (GLA_SUB, V_W)) for r in refs], axis=0)
        gq_e = gq * jnp.exp2(d)
        gk_e = gk * jnp.exp2(-d)
        zero_blk = jnp.zeros((GLA_SUB, V_W), BF16)
        q_sets, k_sets = [], []
        for t2 in range(N_SUB // 2):
            q_rows, k_rows = [], []
            for rb in range(N_SUB):
                s_lo, s_hi = 2 * t2, 2 * t2 + 1
                if s_lo > rb:
                    q_rows.append(zero_blk)
                else:
                    f_lo = jnp.exp2(refs[rb] - refs[s_lo])
                    f_hi = jnp.exp2(refs[rb] - refs[s_hi]) if s_hi <= rb else jnp.zeros_like(f_lo)
                    q_rows.append((blk(gq_e, rb) * jnp.where(dup_half == 0, f_lo, f_hi)).astype(BF16))
                if rb // 2 == t2:
                    k_rows.append(jnp.where(dup_half == rb % 2, blk(gk_e, rb), 0.0).astype(BF16))
                else:
                    k_rows.append(zero_blk)
            q_sets.append(jnp.concatenate(q_rows, axis=0))
            k_sets.append(jnp.concatenate(k_rows, axis=0))

        e_last = jnp.exp2(last)
        out["g_v"] = gv
        out["g_q"] = [jnp.concatenate([qs[:, hd * LANES:(hd + 1) * LANES] for qs in q_sets], axis=-1)
                      for hd in range(HEADS)]
        out["g_k"] = [jnp.concatenate([ks[:, hd * LANES:(hd + 1) * LANES] for ks in k_sets], axis=-1)
                      for hd in range(HEADS)]
        out["g_qe"], out["g_kdt"], out["g_dec"] = [], [], []
        for p in range(HEADS // 2):
            gq_n = natural_pair(gq_e, p)
            gk_n = natural_pair(gk_e, p)
            out["g_qe"].append(jnp.concatenate(
                [blk(gq_n, rb) * natural_pair(jnp.exp2(refs[rb]), p) for rb in range(N_SUB)],
                axis=0).astype(BF16))
            kd = jnp.concatenate(
                [blk(gk_n, rb) * natural_pair(jnp.exp2(last - refs[rb]), p) for rb in range(N_SUB)],
                axis=0)
            out["g_kdt"].append(kd.T.astype(BF16))
            out["g_dec"].append(jnp.broadcast_to(natural_pair(e_last, p), (CHUNK, LANES)).T)
        return out

    def stage_b(c, a):
        rows = slice(c * CHUNK, (c + 1) * CHUNK)
        g_s = [jnp.where(causal, _dot_nt(a["g_q"][hd], a["g_k"][hd]), 0.0).astype(BF16)
               for hd in range(HEADS)]

        v, s_all = a["r_v"], a["r_s"]
        o_inter = _dot(a["r_qd"], rstate_bd_ref[...])
        for p in range(HEADS // 2):
            ha, hb = 2 * p, 2 * p + 1
            s_pair = jnp.concatenate([s_all[ha * CHUNK:(ha + 1) * CHUNK],
                                      s_all[hb * CHUNK:(hb + 1) * CHUNK]], axis=-1)
            o_pair = _dot(s_pair, _block_diag2(v[:, ha * DV:(ha + 1) * DV], v[:, hb * DV:(hb + 1) * DV]))
            for hd in (ha, hb):
                cols = slice(hd * DV, (hd + 1) * DV)
                o = o_pair[:, (hd - ha) * DV:(hd - ha + 1) * DV] + o_inter[:, cols]
                o = _rms(o, ret_g_ref[:, cols])
                gate = z_ref[rows, OFF_RG + hd * DV:OFF_RG + (hd + 1) * DV]
                y_ref[rows, cols] = (o * gate).astype(BF16)
                st = rstate_ref[hd] * cdec_ref[hd] + _dot(a["r_kd"][hd], v[:, cols])
                rstate_ref[hd] = st
                st_bf = st.astype(BF16)
                rstate_bd_ref[hd * hw:(hd + 1) * hw, cols] = st_bf[:hw]
                rstate_bd_ref[half + hd * hw:half + (hd + 1) * hw, cols] = st_bf[hw:]

        gv = a["g_v"]
        for p in range(HEADS // 2):
            ha, hb = 2 * p, 2 * p + 1
            st = gstate_ref[p]
            o_pair = (_dot(a["g_qe"][p], st.astype(BF16))
                      + _dot(jnp.concatenate([g_s[ha], g_s[hb]], axis=-1),
                             _block_diag2(gv[:, ha * DV:(ha + 1) * DV], gv[:, hb * DV:(hb + 1) * DV])))
            dec_col = a["g_dec"][p]
            st = st * jnp.concatenate([dec_col, dec_col], axis=-1)
            gstate_ref[p] = st
            kdt = a["g_kdt"][p]
            for hd in (ha, hb):
                j = hd - ha
                cols = slice(hd * DV, (hd + 1) * DV)
                upd = _dot(kdt[j * DK:(j + 1) * DK], gv[:, cols])
                gstate_ref[p, j * DK:(j + 1) * DK, j * DV:(j + 1) * DV] = (
                    st[j * DK:(j + 1) * DK, j * DV:(j + 1) * DV] + upd)
                o = _rms(o_pair[:, j * DV:(j + 1) * DV], gla_g_ref[:, cols])
                gate = z_ref[rows, OFF_GG + hd * DV:OFF_GG + (hd + 1) * DV]
                y_ref[rows, V_W + hd * DV:V_W + (hd + 1) * DV] = (o * gate).astype(BF16)

    n_sub_chunks = MIX_SUB // CHUNK
    assert tile == 2 * MIX_SUB and n_sub_chunks == N_GROUPS
    rows_a, rows_b = slice(0, MIX_SUB), slice(MIX_SUB, tile)
    pre_norm(rows_a)
    for g in range(N_GROUPS):
        in_projection(rows_a, g)
    pre_norm(rows_b)
    n_chunks = 2 * n_sub_chunks
    fillers = {0: [(in_projection, rows_b, 0), (in_projection, rows_b, 1)],
               1: [(in_projection, rows_b, 2), (in_projection, rows_b, 3)],
               4: [(out_projection, rows_a, 0)],
               5: [(out_projection, rows_a, 1)],
               6: [(out_projection, rows_a, 2)],
               7: [(out_projection, rows_a, 3)]}
    order = {0: [1], 1: [2], 2: [3, 4, 5], 3: [6, 7]}
    ready = {0: stage_a(0)}
    for c in range(n_chunks):
        for nxt in order.get(c, []):
            ready[nxt] = stage_a(nxt)
        stage_b(c, ready.pop(c))
        for fn, srows, g in fillers.get(c, []):
            fn(srows, g)
    post_norm(rows_a)
    for g in range(N_GROUPS):
        out_projection(rows_b, g)
    post_norm(rows_b)


def _mixer(x, layer, cos_t, sin_t, pre_g, w_in, w_a2, b_a2, ret_g, gla_g, w_out, post_g, tables):
    b, t, d = x.shape
    dmat, qdec, kdec, cdec = tables
    const2 = lambda bi, ti: (0, 0)
    lay2 = lambda bi, ti: (layer, 0, 0)
    tok = lambda bi, ti: (bi, ti, 0)
    return pl.pallas_call(
        _mixer_kernel,
        grid=(b, t // MIX_TILE),
        in_specs=[
            pl.BlockSpec((None, MIX_TILE, d), tok),
            pl.BlockSpec((None, MIX_TILE, QK_W // 2), tok),
            pl.BlockSpec((None, MIX_TILE, QK_W // 2), tok),
            pl.BlockSpec((None, 1, d), lay2),
            pl.BlockSpec((None, d, IN_W), lay2, pipeline_mode=pl.Buffered(1)),
            pl.BlockSpec((None, GATE_RANK_PAD, 2 * QK_W), lay2),
            pl.BlockSpec((None, 1, 2 * QK_W), lay2),
            pl.BlockSpec((None, 1, V_W), lay2),
            pl.BlockSpec((None, 1, V_W), lay2),
            pl.BlockSpec((None, 2 * V_W, d), lay2, pipeline_mode=pl.Buffered(1)),
            pl.BlockSpec((None, 1, d), lay2),
            pl.BlockSpec((HEADS * CHUNK, CHUNK), const2),
            pl.BlockSpec((CHUNK, QK_W), const2),
            pl.BlockSpec((CHUNK, QK_W), const2),
            pl.BlockSpec(memory_space=pltpu.SMEM),
        ],
        out_specs=pl.BlockSpec((None, MIX_TILE, d), tok),
        out_shape=jax.ShapeDtypeStruct(x.shape, F32),
        scratch_shapes=[
            pltpu.VMEM((MIX_TILE, d), BF16),
            pltpu.VMEM((MIX_TILE, IN_W), F32),
            pltpu.VMEM((MIX_TILE, 2 * QK_W), F32),
            pltpu.VMEM((MIX_TILE, 2 * V_W), BF16),
            pltpu.VMEM((HEADS, DK, DV), F32),
            pltpu.VMEM((QK_W, V_W), BF16),
            pltpu.VMEM((HEADS // 2, 2 * DK, 2 * DV), F32),
        ],
        compiler_params=pltpu.CompilerParams(
            dimension_semantics=("arbitrary", "arbitrary"), vmem_limit_bytes=VMEM_LIMIT),
        name="mixer",
    )(x, cos_t, sin_t, pre_g, w_in, w_a2, b_a2, ret_g, gla_g, w_out, post_g,
      dmat, qdec, kdec, cdec)


def _retention_tables():
    c = CHUNK
    log_gamma = jnp.log1p(-jnp.exp2(-5.0 - jnp.arange(HEADS, dtype=F32)))
    idx = jnp.arange(c, dtype=F32)
    rel = idx[:, None] - idx[None, :]
    causal = rel >= 0
    decay_in = jnp.where(causal[None],
                         jnp.exp(jnp.where(causal, rel, 0.0)[None] * log_gamma[:, None, None]), 0.0)
    q_dec = jnp.exp((idx + 1.0)[None, :] * log_gamma[:, None])
    k_dec = jnp.exp((c - 1.0 - idx)[None, :] * log_gamma[:, None])
    chunk_dec = jnp.exp(c * log_gamma)
    head_of_lane = (np.arange(QK_W) % (QK_W // 2)) // (DK // 2)
    dmat = decay_in.reshape(HEADS * c, c)
    qdec = q_dec.T[:, head_of_lane]
    kdec = k_dec.T[:, head_of_lane]
    return dmat, qdec, kdec, chunk_dec


def _rotary_cols(w):
    lead = w.shape[:-1]
    w = w.reshape(lead + (HEADS, 2, DK // 2))
    return jnp.swapaxes(w, -3, -2).reshape(lead + (QK_W,))


def _dup_cols(w):
    lead = w.shape[:-1]
    w = w.reshape(lead + (HEADS, 1, DK))
    return jnp.broadcast_to(w, lead + (HEADS, 2, DK)).reshape(lead + (2 * QK_W,))


def _relayout_w_in(w_in):
    sizes = (QK_W, QK_W, V_W, V_W, QK_W, QK_W, V_W, GATE_RANK, V_W)
    pts = np.cumsum(sizes)[:-1]
    rq, rk, rv, rg, gq, gk, gv, ga, gg = jnp.split(w_in, pts, axis=-1)
    ga = jnp.pad(ga, ((0, 0), (0, 0), (0, GATE_RANK_PAD - GATE_RANK)))
    return jnp.concatenate([ga, _dup_cols(gk), _dup_cols(gq) * QK_SCALE, gv,
                            _rotary_cols(rk) * QK_SCALE, _rotary_cols(rq), rv, gg, rg], axis=-1)


def _memkv_kernel(mem_ref, g_ref, w_ref, k_ref, v_ref):
    nb, m, d = mem_ref.shape
    x = mem_ref[...].reshape(nb * m, d)
    kv = _dot(_rms(x, g_ref[...]).astype(BF16), w_ref[...])
    for i in range(nb):
        k_ref[i] = kv[i * m:(i + 1) * m, :D_MODEL].T.astype(BF16)
        v_ref[i] = kv[i * m:(i + 1) * m, D_MODEL:].astype(BF16)


def _memkv(mem, layer, g, w_kv):
    b, m, d = mem.shape
    nb = MEMKV_BATCH
    lay2 = lambda i: (layer, 0, 0)
    return pl.pallas_call(
        _memkv_kernel,
        grid=(b // nb,),
        in_specs=[
            pl.BlockSpec((nb, m, d), lambda i: (i, 0, 0)),
            pl.BlockSpec((None, 1, d), lay2),
            pl.BlockSpec((None, d, 2 * d), lay2),
        ],
        out_specs=[pl.BlockSpec((nb, d, m), lambda i: (i, 0, 0)),
                   pl.BlockSpec((nb, m, d), lambda i: (i, 0, 0))],
        out_shape=[jax.ShapeDtypeStruct((b, d, m), BF16), jax.ShapeDtypeStruct((b, m, d), BF16)],
        compiler_params=pltpu.CompilerParams(
            dimension_semantics=("arbitrary",), vmem_limit_bytes=VMEM_LIMIT,
            allow_input_fusion=[False, False, True]),
        name="memkv",
    )(mem, g, w_kv)


def _xattn_rows(x, rows, kt_ref, v_ref, pre_g_ref, wq_ref, wo_ref, post_g_ref, att_ref):
    h = _rms(x, pre_g_ref[...]).astype(BF16)
    q = _dot(h, wq_ref[...])
    for hd in range(XA_HEADS):
        cols = slice(hd * XA_DH, (hd + 1) * XA_DH)
        s = _dot(q[:, cols].astype(BF16), kt_ref[cols, :])
        p = jnp.exp(s - jnp.max(s, axis=-1, keepdims=True))
        o = _dot(p.astype(BF16), v_ref[:, cols]) / jnp.sum(p, axis=-1, keepdims=True)
        att_ref[rows, cols] = o.astype(BF16)
    out = _dot(att_ref[rows, :], wo_ref[...])
    return x + _rms(out, post_g_ref[...])


def _xattn_kernel(x_ref, kt_ref, v_ref, pre_g_ref, wq_ref, wo_ref, post_g_ref, o_ref, att_ref):
    for sub in range(XA_TILE // XA_SUB):
        rows = slice(sub * XA_SUB, (sub + 1) * XA_SUB)
        o_ref[rows, :] = _xattn_rows(x_ref[rows, :], rows, kt_ref, v_ref, pre_g_ref, wq_ref,
                                     wo_ref, post_g_ref, att_ref)


def _xattn(x, layer, kt, v, pre_g, wq, wo, post_g):
    b, t, d = x.shape
    m = v.shape[1]
    lay2 = lambda bi, ti: (layer, 0, 0)
    tok = lambda bi, ti: (bi, ti, 0)
    mem = lambda bi, ti: (bi, 0, 0)
    return pl.pallas_call(
        _xattn_kernel,
        grid=(b, t // XA_TILE),
        in_specs=[
            pl.BlockSpec((None, XA_TILE, d), tok),
            pl.BlockSpec((None, d, m), mem),
            pl.BlockSpec((None, m, d), mem),
            pl.BlockSpec((None, 1, d), lay2),
            pl.BlockSpec((None, d, d), lay2),
            pl.BlockSpec((None, d, d), lay2),
            pl.BlockSpec((None, 1, d), lay2),
        ],
        out_specs=pl.BlockSpec((None, XA_TILE, d), tok),
        out_shape=jax.ShapeDtypeStruct(x.shape, F32),
        scratch_shapes=[pltpu.VMEM((XA_TILE, d), BF16)],
        compiler_params=pltpu.CompilerParams(
            dimension_semantics=("arbitrary", "arbitrary"), vmem_limit_bytes=VMEM_LIMIT,
            allow_input_fusion=[False, False, False, False, True, True, False]),
        name="xattn",
    )(x, kt, v, pre_g, wq, wo, post_g)


def kernel(x, mem, positions, ffn1_pre_g, ffn1_w_gate, ffn1_w_up, ffn1_w_down, ffn1_post_g, mix_pre_g, w_in, w_a2, b_a2, ret_norm_g, gla_norm_g, w_out, mix_post_g, xa_pre_g, xa_mem_g, xa_w_q, xa_w_kv, xa_w_o, xa_post_g, ffn2_pre_g, ffn2_w_gate, ffn2_w_up, ffn2_w_down, ffn2_post_g):
    b, t, d = x.shape
    depth = w_in.shape[0]
    assert d == D_MODEL and t % MIX_TILE == 0 and t % XA_TILE == 0 and (b * t) % FFN_TILE == 0
    assert (b * t) % ROPE_TILE == 0 and b % MEMKV_BATCH == 0

    cos_t, sin_t = _rope_tables(positions)
    cos_t = cos_t.reshape(b, t, QK_W // 2)
    sin_t = sin_t.reshape(b, t, QK_W // 2)
    tables = _retention_tables()

    row = lambda g: g.reshape(depth, 1, -1)
    bf = lambda w: w.astype(BF16)
    ffn1 = (row(ffn1_pre_g), bf(ffn1_w_gate), bf(ffn1_w_up), bf(ffn1_w_down), row(ffn1_post_g))
    ffn2 = (row(ffn2_pre_g), bf(ffn2_w_gate), bf(ffn2_w_up), bf(ffn2_w_down), row(ffn2_post_g))
    w_a2_p = bf(jnp.pad(_dup_cols(w_a2), ((0, 0), (0, GATE_RANK_PAD - GATE_RANK), (0, 0))))
    mix = (row(mix_pre_g), _relayout_w_in(bf(w_in)), w_a2_p, row(_dup_cols(b_a2)),
           row(ret_norm_g), row(gla_norm_g), bf(w_out), row(mix_post_g))
    xa_kv = (row(xa_mem_g), bf(xa_w_kv))
    xa = (row(xa_pre_g), bf(xa_w_q) * XA_SCALE, bf(xa_w_o), row(xa_post_g))

    for l in range(depth):
        x = _ffn(x.reshape(b * t, d), l, *ffn1).reshape(b, t, d)
        x = _mixer(x, l, cos_t, sin_t, *mix, tables)
        k_mem, v_mem = _memkv(mem, l, *xa_kv)
        x = _xattn(x, l, k_mem, v_mem, *xa)
        x = _ffn(x.reshape(b * t, d), l, *ffn2).reshape(b, t, d)
    return x
```
